```python
import jax, jax.numpy as jnp
from jax import lax
import numpy as np

D_MODEL = 1024
BATCH = 8
SEQ = 2048
DEPTH = 4

MEM_LEN = 256
EPS = 1e-6
ATTN_HEADS = 4
ATTN_HEAD_DIM = 64
ATTN_WIDTH = ATTN_HEADS * ATTN_HEAD_DIM
DILATED_PATTERNS = ((128, 1), (512, 4), (2048, 16))
WINDOW_BLOCK = 128
ROPE_THETA = 500000.0
ROPE_DIM = ATTN_HEAD_DIM // 4
CONV_GROUPS = 4
CONV_WIDTH = D_MODEL // 4
CONV_K = 3
GDN_HEADS = 4
GDN_HEAD_DIM = 128
GDN_WIDTH = GDN_HEADS * GDN_HEAD_DIM
GDN_CONV_K = 4
GDN_CHUNK = 64
MIX_WIDTH = ATTN_WIDTH + CONV_WIDTH + GDN_WIDTH
IN_SPLITS = (ATTN_WIDTH, ATTN_WIDTH, ATTN_WIDTH,
             CONV_WIDTH, CONV_WIDTH, CONV_WIDTH,
             GDN_WIDTH, GDN_WIDTH, GDN_WIDTH, GDN_HEADS, GDN_HEADS, GDN_WIDTH)
IN_WIDTH = 3 * ATTN_WIDTH + 3 * CONV_WIDTH + 4 * GDN_WIDTH + 2 * GDN_HEADS
XATTN_HEADS = 4
XATTN_HEAD_DIM = D_MODEL // XATTN_HEADS
XATTN_WIDTH = XATTN_HEADS * XATTN_HEAD_DIM
FFN_HIDDEN = -(-8 * D_MODEL // (3 * 256)) * 256

kernel_name = 'hybrid_dilated_conv_deltanet_block'


def rms_norm(x, w):
    x32 = x.astype(jnp.float32)
    y = x32 * lax.rsqrt(jnp.mean(x32 * x32, axis=-1, keepdims=True) + EPS)
    return (y * w.astype(jnp.float32)).astype(x.dtype)


def causal_depthwise_conv(x, w):
    K, C = w.shape
    return lax.conv_general_dilated(x, w[:, None, :].astype(x.dtype), window_strides=(1,),
                                    padding=[(K - 1, 0)], dimension_numbers=('NWC', 'WIO', 'NWC'),
                                    feature_group_count=C)


def rotary_tables(positions):
    inv_freq = jnp.float32(ROPE_THETA) ** (-jnp.arange(0, ROPE_DIM, 2, dtype=jnp.float32) / ROPE_DIM)
    ang = positions.astype(jnp.float32)[..., None] * inv_freq
    return jnp.cos(ang)[:, :, None, :], jnp.sin(ang)[:, :, None, :]


def apply_partial_rotary(x, cos, sin):
    half = ROPE_DIM // 2
    x1 = x[..., :half].astype(jnp.float32)
    x2 = x[..., half:ROPE_DIM].astype(jnp.float32)
    rot = jnp.concatenate([x1 * cos - x2 * sin, x2 * cos + x1 * sin], axis=-1).astype(x.dtype)
    return jnp.concatenate([rot, x[..., ROPE_DIM:]], axis=-1)


def dilated_window_attention(q, k, v, dilation, n_back):
    B, S, H, Dh = q.shape
    QB = WINDOW_BLOCK
    L = S // dilation
    nb = -(-L // QB)
    Lp = nb * QB

    def to_residue(t):
        t = t.reshape(B, L, dilation, H, Dh).transpose(0, 2, 1, 3, 4).reshape(B * dilation, L, H, Dh)
        t = jnp.pad(t, ((0, 0), (0, Lp - L), (0, 0), (0, 0)))
        return t.reshape(B * dilation, nb, QB, H, Dh)

    def with_prev(t):
        prev = jnp.pad(t[:, :-1], ((0, 0), (1, 0), (0, 0), (0, 0), (0, 0)))
        return jnp.concatenate([prev, t], axis=2)

    qb = to_residue(q)
    kw = with_prev(to_residue(k))
    vw = with_prev(to_residue(v))
    s = jnp.einsum('bnqhd,bnkhd->bnhqk', qb, kw, preferred_element_type=jnp.float32)
    qi = jnp.arange(nb)[:, None, None] * QB + jnp.arange(QB)[None, :, None]
    kj = jnp.arange(nb)[:, None, None] * QB - QB + jnp.arange(2 * QB)[None, None, :]
    dist = qi - kj
    mask = (dist >= 0) & (dist <= n_back) & (kj >= 0)
    s = jnp.where(mask[None, :, None], s, -jnp.inf)
    m = jnp.max(s, axis=-1, keepdims=True)
    p = jnp.exp(s - m)
    l = jnp.sum(p, axis=-1, keepdims=True)
    o = jnp.einsum('bnhqk,bnkhd->bnqhd', (p / l).astype(v.dtype), vw)
    lse = (m + jnp.log(l))[..., 0]
    o = o.reshape(B, dilation, Lp, H, Dh)[:, :, :L].transpose(0, 2, 1, 3, 4).reshape(B, S, H, Dh)
    lse = lse.transpose(0, 1, 3, 2).reshape(B, dilation, Lp, H)[:, :, :L]
    lse = lse.transpose(0, 2, 1, 3).reshape(B, S, H)
    return o, lse


def dilated_attention_mixer(q, k, v, cos, sin):
    B, S, _ = q.shape
    q = q.reshape(B, S, ATTN_HEADS, ATTN_HEAD_DIM)
    k = k.reshape(B, S, ATTN_HEADS, ATTN_HEAD_DIM)
    v = v.reshape(B, S, ATTN_HEADS, ATTN_HEAD_DIM)
    q = apply_partial_rotary(q, cos, sin) * (ATTN_HEAD_DIM ** -0.5)
    k = apply_partial_rotary(k, cos, sin)
    outs, lses = [], []
    for window, dilation in DILATED_PATTERNS:
        o, lse = dilated_window_attention(q, k, v, dilation, window // dilation)
        outs.append(o.astype(jnp.float32))
        lses.append(lse)
    wts = jax.nn.softmax(jnp.stack(lses, axis=0), axis=0)
    o = jnp.einsum('pbsh,pbshd->bshd', wts, jnp.stack(outs, axis=0))
    return o.reshape(B, S, ATTN_WIDTH).astype(q.dtype)


def short_conv_mixer(b_gate, c_gate, xv, conv_w):
    return b_gate * causal_depthwise_conv(c_gate * xv, conv_w)


def gated_delta_rule(q, k, v, g, beta):
    B, S, H, Dk = q.shape
    Dv = v.shape[-1]
    C = GDN_CHUNK
    N = S // C

    def chunks(t):
        return jnp.moveaxis(t.reshape(B, N, C, H, *t.shape[3:]), 3, 1)

    q, k, v, g, beta = chunks(q), chunks(k), chunks(v), chunks(g), chunks(beta)
    decay = jnp.cumsum(g, axis=-1)
    causal = jnp.tril(jnp.ones((C, C), dtype=bool))
    strict = jnp.tril(jnp.ones((C, C), dtype=bool), -1)
    rel = jnp.exp(jnp.where(causal, decay[..., :, None] - decay[..., None, :], -jnp.inf))
    k_beta = k * beta[..., None]
    a = jnp.where(strict, jnp.einsum('bhnik,bhnjk->bhnij', k_beta, k) * rel, 0.0)
    eye = jnp.broadcast_to(jnp.eye(C, dtype=jnp.float32), a.shape)
    rhs = jnp.concatenate([v * beta[..., None], k_beta * jnp.exp(decay)[..., None]], axis=-1)
    sol = lax.linalg.triangular_solve(eye + a, rhs, left_side=True, lower=True, unit_diagonal=True)
    u, w = sol[..., :Dv], sol[..., Dv:]
    attn = jnp.where(causal, jnp.einsum('bhnik,bhnjk->bhnij', q, k) * rel, 0.0)
    q_dec = q * jnp.exp(decay)[..., None]
    k_dec = k * jnp.exp(decay[..., -1:] - decay)[..., None]
    chunk_decay = jnp.exp(decay[..., -1])

    def step(state, inp):
        q_i, k_i, u_i, w_i, attn_i, cd_i = inp
        v_new = u_i - jnp.einsum('bhck,bhkv->bhcv', w_i, state)
        o_i = (jnp.einsum('bhck,bhkv->bhcv', q_i, state)
               + jnp.einsum('bhij,bhjv->bhiv', attn_i, v_new))
        state = state * cd_i[..., None, None] + jnp.einsum('bhck,bhcv->bhkv', k_i, v_new)
        return state, o_i

    xs = tuple(jnp.moveaxis(t, 2, 0) for t in (q_dec, k_dec, u, w, attn, chunk_decay))
    state0 = jnp.zeros((B, H, Dk, Dv), jnp.float32)
    _, o = lax.scan(step, state0, xs)
    return o.transpose(1, 0, 3, 2, 4).reshape(B, S, H, Dv)


def gated_deltanet_mixer(q, k, v, a, b, gate, conv_w, a_log, dt_bias, norm_w):
    B, S, _ = q.shape
    qkv = jax.nn.silu(causal_depthwise_conv(jnp.concatenate([q, k, v], axis=-1), conv_w))
    q, k, v = jnp.split(qkv.astype(jnp.float32), 3, axis=-1)
    q = q.reshape(B, S, GDN_HEADS, GDN_HEAD_DIM)
    k = k.reshape(B, S, GDN_HEADS, GDN_HEAD_DIM)
    v = v.reshape(B, S, GDN_HEADS, GDN_HEAD_DIM)
    q = q * lax.rsqrt(jnp.sum(q * q, axis=-1, keepdims=True) + EPS) * (GDN_HEAD_DIM ** -0.5)
    k = k * lax.rsqrt(jnp.sum(k * k, axis=-1, keepdims=True) + EPS)
    g = -jnp.exp(a_log.astype(jnp.float32)) * jax.nn.softplus(a.astype(jnp.float32) + dt_bias.astype(jnp.float32))
    beta = jax.nn.sigmoid(b.astype(jnp.float32))
    o = gated_delta_rule(q, k, v, g, beta)
    gate = jax.nn.silu(gate.astype(jnp.float32)).reshape(B, S, GDN_HEADS, GDN_HEAD_DIM)
    o = rms_norm(o, norm_w) * gate
    return o.reshape(B, S, GDN_WIDTH).astype(gate.dtype if gate.dtype == q.dtype else gate.dtype)


def memory_cross_attention(h, mem, w_q, w_kv, w_o):
    B, S, _ = h.shape
    M = mem.shape[1]
    q = (h @ w_q).reshape(B, S, XATTN_HEADS, XATTN_HEAD_DIM)
    k, v = jnp.split(mem @ w_kv, 2, axis=-1)
    k = k.reshape(B, M, XATTN_HEADS, XATTN_HEAD_DIM)
    v = v.reshape(B, M, XATTN_HEADS, XATTN_HEAD_DIM)
    s = jnp.einsum('bshd,bmhd->bhsm', q, k, preferred_element_type=jnp.float32) * (XATTN_HEAD_DIM ** -0.5)
    p = jax.nn.softmax(s, axis=-1)
    o = jnp.einsum('bhsm,bmhd->bshd', p.astype(v.dtype), v).reshape(B, S, XATTN_WIDTH)
    return o @ w_o


def swiglu_ffn(h, w_gate_up, w_down):
    gate, up = jnp.split(h @ w_gate_up, 2, axis=-1)
    return (jax.nn.silu(gate) * up) @ w_down


def setup_inputs(seed: int = 0) -> dict:
    key = jax.random.key(seed)
    ks = jax.random.split(key, 24)
    f32 = jnp.float32

    def dense(k, shape, fan_in):
        return jax.random.normal(k, shape, f32) * (fan_in ** -0.5)

    def gain(k, shape):
        return 1.0 + 0.02 * jax.random.normal(k, shape, f32)

    x = jax.random.normal(ks[0], (BATCH, SEQ, D_MODEL), f32)
    mem = jax.random.normal(ks[1], (BATCH, MEM_LEN, D_MODEL), f32)
    positions = (jax.random.randint(ks[2], (BATCH, 1), 0, 4096, dtype=jnp.int32)
                 + jnp.arange(SEQ, dtype=jnp.int32)[None, :])
    dt = jnp.exp(jax.random.uniform(ks[3], (DEPTH, GDN_HEADS), f32, np.log(1e-3), np.log(1e-1)))
    return {
        'x': x,
        'mem': mem,
        'positions': positions,
        'norm_mix_pre': gain(ks[4], (DEPTH, D_MODEL)),
        'norm_mix_post': gain(ks[5], (DEPTH, D_MODEL)),
        'w_in': dense(ks[6], (DEPTH, D_MODEL, IN_WIDTH), D_MODEL),
        'conv_short': dense(ks[7], (DEPTH, CONV_K, CONV_WIDTH), CONV_K),
        'conv_gdn': dense(ks[8], (DEPTH, GDN_CONV_K, 3 * GDN_WIDTH), GDN_CONV_K),
        'gdn_a_log': jnp.log(jax.random.uniform(ks[9], (DEPTH, GDN_HEADS), f32, 1.0, 16.0)),
        'gdn_dt_bias': dt + jnp.log(-jnp.expm1(-dt)),
        'gdn_norm': gain(ks[10], (DEPTH, GDN_HEAD_DIM)),
        'w_out': dense(ks[11], (DEPTH, MIX_WIDTH, D_MODEL), MIX_WIDTH),
        'norm_mem': gain(ks[12], (DEPTH, D_MODEL)),
        'norm_xattn_pre': gain(ks[13], (DEPTH, D_MODEL)),
        'norm_xattn_post': gain(ks[14], (DEPTH, D_MODEL)),
        'w_xq': dense(ks[15], (DEPTH, D_MODEL, XATTN_WIDTH), D_MODEL),
        'w_xkv': dense(ks[16], (DEPTH, D_MODEL, 2 * XATTN_WIDTH), D_MODEL),
        'w_xo': dense(ks[17], (DEPTH, XATTN_WIDTH, D_MODEL), XATTN_WIDTH),
        'norm_ffn_pre': gain(ks[18], (DEPTH, D_MODEL)),
        'norm_ffn_post': gain(ks[19], (DEPTH, D_MODEL)),
        'w_gate_up': dense(ks[20], (DEPTH, D_MODEL, 2 * FFN_HIDDEN), D_MODEL),
        'w_down': dense(ks[21], (DEPTH, FFN_HIDDEN, D_MODEL), FFN_HIDDEN),
    }


def reference(x, mem, positions, norm_mix_pre, norm_mix_post, w_in, conv_short, conv_gdn,
              gdn_a_log, gdn_dt_bias, gdn_norm, w_out, norm_mem, norm_xattn_pre, norm_xattn_post,
              w_xq, w_xkv, w_xo, norm_ffn_pre, norm_ffn_post, w_gate_up, w_down):
    cos, sin = rotary_tables(positions)
    split_idx = [int(i) for i in np.cumsum(IN_SPLITS)[:-1]]
    h = x
    for l in range(DEPTH):
        hn = rms_norm(h, norm_mix_pre[l])
        proj = hn @ w_in[l]
        (aq, ak, av, cb, cc, cx, gq, gk, gv, ga, gb, gg) = jnp.split(proj, split_idx, axis=-1)
        y_attn = dilated_attention_mixer(aq, ak, av, cos, sin)
        y_conv = short_conv_mixer(cb, cc, cx, conv_short[l])
        y_gdn = gated_deltanet_mixer(gq, gk, gv, ga, gb, gg, conv_gdn[l],
                                     gdn_a_log[l], gdn_dt_bias[l], gdn_norm[l]).astype(proj.dtype)
        mix = jnp.concatenate([y_attn.astype(proj.dtype), y_conv, y_gdn], axis=-1) @ w_out[l]
        h = h + rms_norm(mix, norm_mix_post[l])
        hn = rms_norm(h, norm_xattn_pre[l])
        xa = memory_cross_attention(hn, rms_norm(mem, norm_mem[l]), w_xq[l], w_xkv[l], w_xo[l])
        h = h + rms_norm(xa, norm_xattn_post[l])
        hn = rms_norm(h, norm_ffn_pre[l])
        h = h + rms_norm(swiglu_ffn(hn, w_gate_up[l], w_down[l]), norm_ffn_post[l])
    return h
```

```python
import functools

import jax
import jax.numpy as jnp
from jax import lax
from jax.experimental import pallas as pl
from jax.experimental.pallas import tpu as pltpu

F32 = jnp.float32
BF16 = jnp.bfloat16
EPS = 1e-6

ATTN_HEADS = 4
ATTN_HEAD_DIM = 64
ATTN_WIDTH = ATTN_HEADS * ATTN_HEAD_DIM
DILATED_PATTERNS = ((128, 1), (512, 4), (2048, 16))
QUERY_BLOCK = 128
ROPE_THETA = 500000.0
ROPE_DIM = ATTN_HEAD_DIM // 4
CONV_WIDTH = 256
CONV_K = 3
GDN_HEADS = 4
GDN_HEAD_DIM = 128
GDN_WIDTH = GDN_HEADS * GDN_HEAD_DIM
GDN_CONV_K = 4
GDN_CHUNK = 64
GDN_SUB = 16
XATTN_HEADS = 4
LANES = 128
ATTN_PLANES = ATTN_WIDTH // LANES
HEADS_PER_PLANE = LANES // ATTN_HEAD_DIM
HALO = 8
IN_MAIN = 3 * ATTN_WIDTH + 3 * CONV_WIDTH + 4 * GDN_WIDTH
IN_PAD = IN_MAIN + LANES
VMEM_LIMIT = 56 * 1024 * 1024


def _params(n_grid):
    return pltpu.CompilerParams(dimension_semantics=("arbitrary",) * n_grid,
                                vmem_limit_bytes=VMEM_LIMIT)


def _rms(x, gain):
    return x * lax.rsqrt(jnp.mean(x * x, axis=-1, keepdims=True) + EPS) * gain


def _dot(a, b):
    return jnp.dot(a, b, preferred_element_type=F32)


def _dot_nt(a, b):
    return lax.dot_general(a, b, (((1,), (1,)), ((), ())), preferred_element_type=F32)


def _dot_f32(a, b):
    return jnp.dot(a, b, preferred_element_type=F32, precision=lax.Precision.HIGHEST)


def _silu(x):
    return x * (1.0 / (1.0 + jnp.exp(-x)))


def _resident(shape):
    return pl.BlockSpec(shape, lambda *_: (0,) * len(shape))


def _norm_matmul_kernel(x_ref, g_ref, w_ref, o_ref, *, col_chunk):
    xn = _rms(x_ref[...], g_ref[...]).astype(BF16)
    n = w_ref.shape[1]
    for c in range(0, n, col_chunk):
        e = min(c + col_chunk, n)
        o_ref[:, c:e] = _dot(xn, w_ref[:, c:e]).astype(o_ref.dtype)


def norm_matmul(x, gain, w, *, out_dtype, row_tile=512, col_chunk=512):
    t, d = x.shape
    n = w.shape[1]
    return pl.pallas_call(
        functools.partial(_norm_matmul_kernel, col_chunk=col_chunk),
        grid=(t // row_tile,),
        in_specs=[pl.BlockSpec((row_tile, d), lambda i: (i, 0)), _resident((1, d)), _resident((d, n))],
        out_specs=pl.BlockSpec((row_tile, n), lambda i: (i, 0)),
        out_shape=jax.ShapeDtypeStruct((t, n), out_dtype),
        compiler_params=_params(1),
        name="norm_matmul",
    )(x, gain.reshape(1, d), w)


def _attn_kernel(qkv_ref, cos_ref, sa_ref, sb_ref, o_ref, qr_ref, kr_ref, vr_ref, op_ref, lse_ref, *, row_chunk):
    seq = qkv_ref.shape[1]
    qb = QUERY_BLOCK

    def rotate(i, carry):
        rows = pl.ds(pl.multiple_of(i * row_chunk, row_chunk), row_chunk)
        c, sa, sb = cos_ref[0, rows, :], sa_ref[0, rows, :], sb_ref[0, rows, :]
        for pln in range(ATTN_PLANES):
            for src, dst, scale in ((0, qr_ref, ATTN_HEAD_DIM ** -0.5), (ATTN_WIDTH, kr_ref, 1.0)):
                x = qkv_ref[0, rows, src + pln * LANES:src + (pln + 1) * LANES]
                rot = (x * c + pltpu.roll(x, LANES - ROPE_DIM // 2, 1) * sa
                       + pltpu.roll(x, ROPE_DIM // 2, 1) * sb)
                dst[pln, rows, :] = rot * scale
            vr_ref[pln, rows, :] = qkv_ref[0, rows, 2 * ATTN_WIDTH + pln * LANES:2 * ATTN_WIDTH + (pln + 1) * LANES]
        return carry

    lax.fori_loop(0, seq // row_chunk, rotate, 0)

    lane = lax.broadcasted_iota(jnp.int32, (1, LANES), 1)
    head_masks = [(lane >= ATTN_HEAD_DIM * h) & (lane < ATTN_HEAD_DIM * (h + 1)) for h in range(HEADS_PER_PLANE)]
    qi = lax.broadcasted_iota(jnp.int32, (qb, 2 * qb), 0)
    kj = lax.broadcasted_iota(jnp.int32, (qb, 2 * qb), 1)
    dist = qi + qb - kj

    for p, (window, dil) in enumerate(DILATED_PATTERNS):
        n_back = window // dil
        nb = seq // dil // qb
        band = (dist >= 0) & (dist <= n_back)

        def block(it, carry, p=p, dil=dil, nb=nb, band=band):
            r = it // nb
            n = it % nb
            cur = r + n * (qb * dil)
            prv = r + jnp.maximum(n - 1, 0) * (qb * dil)
            if dil == 1:
                rows_c, rows_p = pl.ds(pl.multiple_of(cur, qb), qb), pl.ds(pl.multiple_of(prv, qb), qb)
            else:
                rows_c, rows_p = pl.ds(cur, qb, stride=dil), pl.ds(prv, qb, stride=dil)
            mask = band & ((kj >= qb) | (n > 0))
            for pln in range(ATTN_PLANES):
                q = qr_ref[pln, rows_c, :]
                k = jnp.concatenate([kr_ref[pln, rows_p, :], kr_ref[pln, rows_c, :]], axis=0).astype(BF16)
                v = jnp.concatenate([vr_ref[pln, rows_p, :], vr_ref[pln, rows_c, :]], axis=0).astype(BF16)
                o_acc = jnp.zeros((qb, LANES), F32)
                lse_acc = jnp.zeros((qb, LANES), F32)
                for h in range(HEADS_PER_PLANE):
                    qh = jnp.where(head_masks[h], q, 0.0).astype(BF16)
                    s = jnp.where(mask, _dot_nt(qh, k), -jnp.inf)
                    m = jnp.max(s, axis=-1, keepdims=True)
                    pexp = jnp.exp(s - m)
                    l = jnp.sum(pexp, axis=-1, keepdims=True)
                    oh = _dot(pexp.astype(BF16), v)
                    o_acc = jnp.where(head_masks[h], oh / l, o_acc)
                    lse_acc = jnp.where(head_masks[h], m + jnp.log(l), lse_acc)
                op_ref[p, pln, rows_c, :] = o_acc
                lse_ref[p, pln, rows_c, :] = lse_acc
            return carry

        lax.fori_loop(0, dil * nb, block, 0)

    def combine(i, carry):
        rows = pl.ds(pl.multiple_of(i * row_chunk, row_chunk), row_chunk)
        for pln in range(ATTN_PLANES):
            lses = [lse_ref[p, pln, rows, :] for p in range(len(DILATED_PATTERNS))]
            m = functools.reduce(jnp.maximum, lses)
            es = [jnp.exp(x - m) for x in lses]
            num = sum(e * op_ref[p, pln, rows, :] for p, e in enumerate(es))
            o_ref[0, rows, pln * LANES:(pln + 1) * LANES] = (num / sum(es)).astype(o_ref.dtype)
        return carry

    lax.fori_loop(0, seq // row_chunk, combine, 0)


def dilated_attention(proj, cos_t, sa_t, sb_t):
    b, s, _ = proj.shape
    assert all(s % (QUERY_BLOCK * d) == 0 and w // d <= QUERY_BLOCK for w, d in DILATED_PATTERNS)
    n_pat = len(DILATED_PATTERNS)
    tab = pl.BlockSpec((1, s, LANES), lambda i: (i, 0, 0))
    plane = pltpu.VMEM((ATTN_PLANES, s, LANES), F32)
    stats = pltpu.VMEM((n_pat, ATTN_PLANES, s, LANES), F32)
    return pl.pallas_call(
        functools.partial(_attn_kernel, row_chunk=256),
        grid=(b,),
        in_specs=[pl.BlockSpec((1, s, 3 * ATTN_WIDTH), lambda i: (i, 0, 0)), tab, tab, tab],
        out_specs=pl.BlockSpec((1, s, ATTN_WIDTH), lambda i: (i, 0, 0)),
        out_shape=jax.ShapeDtypeStruct((b, s, ATTN_WIDTH), BF16),
        scratch_shapes=[plane, plane, plane, stats, stats],
        compiler_params=_params(1),
        name="dilated_attention",
    )(proj, cos_t, sa_t, sb_t)


def _sconv_kernel(x_ref, w_ref, o_ref, win_ref, *, row_chunk):
    seq = x_ref.shape[1]
    cw = CONV_WIDTH

    def body(i, carry):
        start = pl.multiple_of(i * row_chunk, row_chunk)
        pstart = pl.multiple_of(jnp.maximum(start - HALO, 0), HALO)
        rows, prows = pl.ds(start, row_chunk), pl.ds(pstart, HALO)
        zp = x_ref[0, prows, cw:2 * cw] * x_ref[0, prows, 2 * cw:3 * cw]
        win_ref[0:HALO, :] = jnp.where(i > 0, zp, 0.0)
        win_ref[HALO:HALO + row_chunk, :] = x_ref[0, rows, cw:2 * cw] * x_ref[0, rows, 2 * cw:3 * cw]
        y = sum(w_ref[j:j + 1, :] * win_ref[pl.ds(HALO - (CONV_K - 1) + j, row_chunk), :]
                for j in range(CONV_K))
        o_ref[0, rows, :] = (x_ref[0, rows, 0:cw] * y).astype(o_ref.dtype)
        return carry

    lax.fori_loop(0, seq // row_chunk, body, 0)


def short_conv(proj, conv_w, row_chunk=256):
    b, s, _ = proj.shape
    return pl.pallas_call(
        functools.partial(_sconv_kernel, row_chunk=row_chunk),
        grid=(b,),
        in_specs=[pl.BlockSpec((1, s, 3 * CONV_WIDTH), lambda i: (i, 0, 1)), _resident((CONV_K, CONV_WIDTH))],
        out_specs=pl.BlockSpec((1, s, CONV_WIDTH), lambda i: (i, 0, 0)),
        out_shape=jax.ShapeDtypeStruct((b, s, CONV_WIDTH), BF16),
        scratch_shapes=[pltpu.VMEM((HALO + row_chunk, CONV_WIDTH), F32)],
        compiler_params=_params(1),
        name="short_conv",
    )(proj, conv_w)


def _unit_lower_inverse(a, eye, diag_blocks):
    c = a.shape[0]
    d = jnp.where(diag_blocks, a, 0.0)
    t = eye - d
    pw = _dot_f32(d, d)
    for _ in range(GDN_SUB.bit_length() - 3):
        t = t + _dot_f32(t, pw)
        pw = _dot_f32(pw, pw)
    t = t + _dot_f32(t, pw)
    assert c // GDN_SUB == 4
    m = _dot_f32(t, a - d)
    m2 = _dot_f32(m, m)
    return _dot_f32(_dot_f32(eye - m, eye + m2), t)


def _gdn_kernel(qkv_ref, gate_ref, ab_ref, cw_ref, alog_ref, dtb_ref, nw_ref, o_ref, win_ref, state_ref):
    seq = qkv_ref.shape[1]
    c = GDN_CHUNK
    dh = GDN_HEAD_DIM
    row = lax.broadcasted_iota(jnp.int32, (c, c), 0)
    col = lax.broadcasted_iota(jnp.int32, (c, c), 1)
    causal = row >= col
    strict = row > col
    diag_blocks = (row // GDN_SUB) == (col // GDN_SUB)
    eye = (row == col).astype(F32)
    tril_ones = causal.astype(F32)
    state_ref[...] = jnp.zeros_like(state_ref)

    def chunk(n, carry):
        start = pl.multiple_of(n * c, c)
        pstart = pl.multiple_of(jnp.maximum(start - HALO, 0), HALO)
        rows = pl.ds(start, c)
        win_ref[0:HALO, :] = jnp.where(n > 0, qkv_ref[0, pl.ds(pstart, HALO), :], 0.0)
        win_ref[HALO:HALO + c, :] = qkv_ref[0, rows, :]
        xc = sum(cw_ref[j:j + 1, :] * win_ref[pl.ds(HALO - (GDN_CONV_K - 1) + j, c), :]
                 for j in range(GDN_CONV_K))
        xc = _silu(xc)

        ab = ab_ref[0, rows, :]
        z = ab + dtb_ref[...]
        softplus = jnp.maximum(z, 0.0) + jnp.log1p(jnp.exp(-jnp.abs(z)))
        g_all = -jnp.exp(alog_ref[...]) * softplus
        beta_all = 1.0 / (1.0 + jnp.exp(-ab))
        decay_all = _dot_f32(tril_ones, g_all)
        decay_t = decay_all.T

        for h in range(GDN_HEADS):
            q = xc[:, h * dh:(h + 1) * dh]
            k = xc[:, GDN_WIDTH + h * dh:GDN_WIDTH + (h + 1) * dh]
            v = xc[:, 2 * GDN_WIDTH + h * dh:2 * GDN_WIDTH + (h + 1) * dh]
            q = q * lax.rsqrt(jnp.sum(q * q, axis=-1, keepdims=True) + EPS) * (dh ** -0.5)
            k = k * lax.rsqrt(jnp.sum(k * k, axis=-1, keepdims=True) + EPS)
            dcol = decay_all[:, h:h + 1]
            drow = decay_t[h:h + 1, :]
            dlast = decay_all[c - 1:c, h:h + 1]
            beta = beta_all[:, GDN_HEADS + h:GDN_HEADS + h + 1]
            rel = jnp.exp(jnp.where(causal, dcol - drow, -jnp.inf))
            edec = jnp.exp(dcol)
            kb = k * beta
            k16 = k.astype(BF16)
            a = jnp.where(strict, _dot_nt(kb.astype(BF16), k16) * rel, 0.0)
            t = _unit_lower_inverse(a, eye, diag_blocks)
            u = _dot_f32(t, v * beta)
            w = _dot_f32(t, kb * edec)
            attn = jnp.where(causal, _dot_nt(q.astype(BF16), k16) * rel, 0.0)
            q_dec = q * edec
            k_dec = k * jnp.exp(dlast - dcol)
            state = state_ref[h]
            s16 = state.astype(BF16)
            v_new = u - _dot(w.astype(BF16), s16)
            v16 = v_new.astype(BF16)
            o = _dot(q_dec.astype(BF16), s16) + _dot(attn.astype(BF16), v16)
            state_ref[h] = state * jnp.exp(dlast) + _dot(k_dec.T.astype(BF16), v16)
            gate = gate_ref[0, rows, h * dh:(h + 1) * dh]
            o_ref[0, rows, h * dh:(h + 1) * dh] = (_rms(o, nw_ref[...]) * _silu(gate)).astype(o_ref.dtype)
        return carry

    lax.fori_loop(0, seq // c, chunk, 0)


def gated_deltanet(proj, conv_w, a_log, dt_bias, norm_w):
    b, s, _ = proj.shape
    pad = jnp.zeros((LANES - GDN_HEADS,), F32)
    alog_v = jnp.concatenate([a_log.astype(F32), pad]).reshape(1, LANES)
    dtb_v = jnp.concatenate([dt_bias.astype(F32), pad]).reshape(1, LANES)
    qkv_w = 3 * GDN_WIDTH
    return pl.pallas_call(
        _gdn_kernel,
        grid=(b,),
        in_specs=[pl.BlockSpec((1, s, qkv_w), lambda i: (i, 0, 1)),
                  pl.BlockSpec((1, s, GDN_WIDTH), lambda i: (i, 0, (qkv_w * 2) // GDN_WIDTH)),
                  pl.BlockSpec((1, s, LANES), lambda i: (i, 0, IN_MAIN // LANES)),
                  _resident((GDN_CONV_K, qkv_w)), _resident((1, LANES)), _resident((1, LANES)),
                  _resident((1, GDN_HEAD_DIM))],
        out_specs=pl.BlockSpec((1, s, GDN_WIDTH), lambda i: (i, 0, 0)),
        out_shape=jax.ShapeDtypeStruct((b, s, GDN_WIDTH), BF16),
        scratch_shapes=[pltpu.VMEM((HALO + GDN_CHUNK, qkv_w), F32),
                        pltpu.VMEM((GDN_HEADS, GDN_HEAD_DIM, GDN_HEAD_DIM), F32)],
        compiler_params=_params(1),
        name="gated_deltanet",
    )(proj, proj, proj, conv_w, alog_v, dtb_v, norm_w.reshape(1, GDN_HEAD_DIM))


def _out_proj_kernel(ya_ref, yc_ref, yg_ref, w_ref, g_ref, h_ref, o_ref):
    a_w, c_w = ya_ref.shape[1], yc_ref.shape[1]
    mix = (_dot(ya_ref[...], w_ref[0:a_w, :]) + _dot(yc_ref[...], w_ref[a_w:a_w + c_w, :])
           + _dot(yg_ref[...], w_ref[a_w + c_w:, :]))
    o_ref[...] = h_ref[...] + _rms(mix, g_ref[...])


def out_proj(ya, yc, yg, w, gain, h, row_tile=512):
    t, d = h.shape
    rows = lambda width: pl.BlockSpec((row_tile, width), lambda i: (i, 0))
    return pl.pallas_call(
        _out_proj_kernel,
        grid=(t // row_tile,),
        in_specs=[rows(ya.shape[1]), rows(yc.shape[1]), rows(yg.shape[1]), _resident(w.shape),
                  _resident((1, d)), rows(d)],
        out_specs=rows(d),
        out_shape=jax.ShapeDtypeStruct((t, d), F32),
        compiler_params=_params(1),
        name="out_proj",
    )(ya, yc, yg, w, gain.reshape(1, d), h)


def _xattn_kernel(h_ref, kv_ref, wq_ref, wo_ref, gpre_ref, gpost_ref, o_ref):
    h = h_ref[0]
    d = h.shape[-1]
    dh = d // XATTN_HEADS
    q = _dot(_rms(h, gpre_ref[...]).astype(BF16), wq_ref[...]).astype(BF16)
    heads = []
    for i in range(XATTN_HEADS):
        k = kv_ref[0, :, i * dh:(i + 1) * dh]
        v = kv_ref[0, :, d + i * dh:d + (i + 1) * dh]
        s = _dot_nt(q[:, i * dh:(i + 1) * dh], k) * (dh ** -0.5)
        p = jnp.exp(s - jnp.max(s, axis=-1, keepdims=True))
        l = jnp.sum(p, axis=-1, keepdims=True)
        heads.append((_dot(p.astype(BF16), v) / l).astype(BF16))
    xa = _dot(jnp.concatenate(heads, axis=-1), wo_ref[...])
    o_ref[0] = h + _rms(xa, gpost_ref[...])


def cross_attention(h, kv, wq, wo, g_pre, g_post, row_tile=512):
    b, s, d = h.shape
    m = kv.shape[1]
    return pl.pallas_call(
        _xattn_kernel,
        grid=(b, s // row_tile),
        in_specs=[pl.BlockSpec((1, row_tile, d), lambda i, j: (i, j, 0)),
                  pl.BlockSpec((1, m, 2 * d), lambda i, j: (i, 0, 0)),
                  _resident((d, d)), _resident((d, d)), _resident((1, d)), _resident((1, d))],
        out_specs=pl.BlockSpec((1, row_tile, d), lambda i, j: (i, j, 0)),
        out_shape=jax.ShapeDtypeStruct((b, s, d), F32),
        compiler_params=_params(2),
        name="cross_attention",
    )(h, kv, wq, wo, g_pre.reshape(1, d), g_post.reshape(1, d))


def _ffn_kernel(h_ref, wgu_ref, wd_ref, gpre_ref, gpost_ref, o_ref, acc_ref, *, hid_chunk):
    h = h_ref[...]
    hidden = wd_ref.shape[0]
    hn = _rms(h, gpre_ref[...]).astype(BF16)
    for idx, c in enumerate(range(0, hidden, hid_chunk)):
        e = min(c + hid_chunk, hidden)
        gate = _dot(hn, wgu_ref[:, c:e])
        up = _dot(hn, wgu_ref[:, hidden + c:hidden + e])
        part = _dot((_silu(gate) * up).astype(BF16), wd_ref[c:e, :])
        if idx == 0:
            acc_ref[...] = part
        else:
            acc_ref[...] += part
    o_ref[...] = h + _rms(acc_ref[...], gpost_ref[...])


def swiglu_ffn(h, wgu, wd, g_pre, g_post, row_tile=512, hid_chunk=512):
    t, d = h.shape
    rows = pl.BlockSpec((row_tile, d), lambda i: (i, 0))
    return pl.pallas_call(
        functools.partial(_ffn_kernel, hid_chunk=hid_chunk),
        grid=(t // row_tile,),
        in_specs=[rows, _resident(wgu.shape), _resident(wd.shape), _resident((1, d)), _resident((1, d))],
        out_specs=rows,
        out_shape=jax.ShapeDtypeStruct((t, d), F32),
        scratch_shapes=[pltpu.VMEM((row_tile, d), F32)],
        compiler_params=_params(1),
        name="swiglu_ffn",
    )(h, wgu, wd, g_pre.reshape(1, d), g_post.reshape(1, d))


def _rotary_tables(positions):
    half = ROPE_DIM // 2
    inv_freq = jnp.float32(ROPE_THETA) ** (-jnp.arange(0, ROPE_DIM, 2, dtype=F32) / ROPE_DIM)
    ang = positions.astype(F32)[..., None] * inv_freq
    cos, sin = jnp.cos(ang), jnp.sin(ang)
    rest = ATTN_HEAD_DIM - ROPE_DIM
    ones = jnp.ones(ang.shape[:-1] + (rest,), F32)
    zeros = jnp.zeros(ang.shape[:-1] + (rest,), F32)
    zh = jnp.zeros_like(sin)
    reps = LANES // ATTN_HEAD_DIM
    tile = lambda parts: jnp.tile(jnp.concatenate(parts, axis=-1), (1, 1, reps))
    return tile([cos, cos, ones]), tile([-sin, zh, zeros]), tile([zh, sin, zeros])


def _arrange_w_in(w):
    ab0 = IN_MAIN - GDN_WIDTH
    ab1 = ab0 + 2 * GDN_HEADS
    pad = jnp.zeros((w.shape[0], LANES - 2 * GDN_HEADS), w.dtype)
    return jnp.concatenate([w[:, :ab0], w[:, ab1:], w[:, ab0:ab1], pad], axis=1).astype(BF16)


def kernel(x, mem, positions, norm_mix_pre, norm_mix_post, w_in, conv_short, conv_gdn, gdn_a_log, gdn_dt_bias, gdn_norm, w_out, norm_mem, norm_xattn_pre, norm_xattn_post, w_xq, w_xkv, w_xo, norm_ffn_pre, norm_ffn_post, w_gate_up, w_down):
    b, s, d = x.shape
    m = mem.shape[1]
    depth = w_in.shape[0]
    cos_t, sa_t, sb_t = _rotary_tables(positions)
    mem2 = mem.reshape(b * m, d)
    h = x.reshape(b * s, d)
    for l in range(depth):
        w_in_l = _arrange_w_in(w_in[l])
        proj = norm_matmul(h, norm_mix_pre[l], w_in_l, out_dtype=F32).reshape(b, s, IN_PAD)
        ya = dilated_attention(proj, cos_t, sa_t, sb_t)
        yc = short_conv(proj, conv_short[l])
        yg = gated_deltanet(proj, conv_gdn[l], gdn_a_log[l], gdn_dt_bias[l], gdn_norm[l])
        h = out_proj(ya.reshape(b * s, -1), yc.reshape(b * s, -1), yg.reshape(b * s, -1),
                     w_out[l].astype(BF16), norm_mix_post[l], h)
        kv = norm_matmul(mem2, norm_mem[l], w_xkv[l].astype(BF16), out_dtype=BF16).reshape(b, m, 2 * d)
        h = cross_attention(h.reshape(b, s, d), kv, w_xq[l].astype(BF16), w_xo[l].astype(BF16),
                            norm_xattn_pre[l], norm_xattn_post[l]).reshape(b * s, d)
        h = swiglu_ffn(h, w_gate_up[l].astype(BF16), w_down[l].astype(BF16),
                       norm_ffn_pre[l], norm_ffn_post[l])
    return h.reshape(b, s, d)
```

```python
import functools

import jax
import jax.numpy as jnp
from jax import lax
from jax.experimental import pallas as pl
from jax.experimental.pallas import tpu as pltpu

F32 = jnp.float32
BF16 = jnp.bfloat16
EPS = 1e-6

ATTN_HEADS = 4
ATTN_HEAD_DIM = 64
ATTN_WIDTH = ATTN_HEADS * ATTN_HEAD_DIM
DILATED_PATTERNS = ((128, 1), (512, 4), (2048, 16))
QUERY_BLOCK = 128
ROPE_THETA = 500000.0
ROPE_DIM = ATTN_HEAD_DIM // 4
CONV_WIDTH = 256
CONV_K = 3
GDN_HEADS = 4
GDN_HEAD_DIM = 128
GDN_WIDTH = GDN_HEADS * GDN_HEAD_DIM
GDN_CONV_K = 4
GDN_CHUNK = 64
GDN_SUB = 16
XATTN_HEADS = 4
LANES = 128
ATTN_PLANES = ATTN_WIDTH // LANES
HEADS_PER_PLANE = LANES // ATTN_HEAD_DIM
HALO = 8
IN_MAIN = 3 * ATTN_WIDTH + 3 * CONV_WIDTH + 4 * GDN_WIDTH
IN_PAD = IN_MAIN + LANES
VMEM_LIMIT = 56 * 1024 * 1024


def _params(n_grid):
    return pltpu.CompilerParams(dimension_semantics=("arbitrary",) * n_grid,
                                vmem_limit_bytes=VMEM_LIMIT)


def _rms(x, gain):
    return x * lax.rsqrt(jnp.mean(x * x, axis=-1, keepdims=True) + EPS) * gain


def _dot(a, b):
    return jnp.dot(a, b, preferred_element_type=F32)


def _dot_nt(a, b):
    return lax.dot_general(a, b, (((1,), (1,)), ((), ())), preferred_element_type=F32)


def _silu(x):
    return x * (1.0 / (1.0 + jnp.exp(-x)))


def _resident(shape):
    return pl.BlockSpec(shape, lambda *_: (0,) * len(shape))


def _norm_matmul_kernel(x_ref, g_ref, w_ref, o_ref, *, col_chunk):
    xn = _rms(x_ref[...], g_ref[...]).astype(BF16)
    n = w_ref.shape[1]
    for c in range(0, n, col_chunk):
        e = min(c + col_chunk, n)
        o_ref[:, c:e] = _dot(xn, w_ref[:, c:e]).astype(o_ref.dtype)


def norm_matmul(x, gain, w, *, out_dtype, row_tile=512, col_chunk=512):
    t, d = x.shape
    n = w.shape[1]
    return pl.pallas_call(
        functools.partial(_norm_matmul_kernel, col_chunk=col_chunk),
        grid=(t // row_tile,),
        in_specs=[pl.BlockSpec((row_tile, d), lambda i: (i, 0)), _resident((1, d)), _resident((d, n))],
        out_specs=pl.BlockSpec((row_tile, n), lambda i: (i, 0)),
        out_shape=jax.ShapeDtypeStruct((t, n), out_dtype),
        compiler_params=_params(1),
        name="norm_matmul",
    )(x, gain.reshape(1, d), w)


def _attn_kernel(qkv_ref, cos_ref, sa_ref, sb_ref, o_ref, qr_ref, kr_ref, vr_ref, op_ref, lse_ref, *, row_chunk):
    seq = qkv_ref.shape[1]
    qb = QUERY_BLOCK

    def rotate(i, carry):
        rows = pl.ds(pl.multiple_of(i * row_chunk, row_chunk), row_chunk)
        c, sa, sb = cos_ref[0, rows, :], sa_ref[0, rows, :], sb_ref[0, rows, :]
        for pln in range(ATTN_PLANES):
            for src, dst, scale in ((0, qr_ref, ATTN_HEAD_DIM ** -0.5), (ATTN_WIDTH, kr_ref, 1.0)):
                x = qkv_ref[0, rows, src + pln * LANES:src + (pln + 1) * LANES]
                rot = (x * c + pltpu.roll(x, LANES - ROPE_DIM // 2, 1) * sa
                       + pltpu.roll(x, ROPE_DIM // 2, 1) * sb)
                dst[pln, rows, :] = rot * scale
            vr_ref[pln, rows, :] = qkv_ref[0, rows, 2 * ATTN_WIDTH + pln * LANES:2 * ATTN_WIDTH + (pln + 1) * LANES]
        return carry

    lax.fori_loop(0, seq // row_chunk, rotate, 0)

    lane = lax.broadcasted_iota(jnp.int32, (1, LANES), 1)
    head_masks = [(lane >= ATTN_HEAD_DIM * h) & (lane < ATTN_HEAD_DIM * (h + 1)) for h in range(HEADS_PER_PLANE)]
    qi = lax.broadcasted_iota(jnp.int32, (qb, 2 * qb), 0)
    kj = lax.broadcasted_iota(jnp.int32, (qb, 2 * qb), 1)
    dist = qi + qb - kj

    for p, (window, dil) in enumerate(DILATED_PATTERNS):
        n_back = window // dil
        nb = seq // dil // qb
        band = (dist >= 0) & (dist <= n_back)

        def block(it, carry, p=p, dil=dil, nb=nb, band=band):
            r = it // nb
            n = it % nb
            cur = r + n * (qb * dil)
            prv = r + jnp.maximum(n - 1, 0) * (qb * dil)
            if dil == 1:
                rows_c, rows_p = pl.ds(pl.multiple_of(cur, qb), qb), pl.ds(pl.multiple_of(prv, qb), qb)
            else:
                rows_c, rows_p = pl.ds(cur, qb, stride=dil), pl.ds(prv, qb, stride=dil)
            mask = band & ((kj >= qb) | (n > 0))
            for pln in range(ATTN_PLANES):
                q = qr_ref[pln, rows_c, :]
                k = jnp.concatenate([kr_ref[pln, rows_p, :], kr_ref[pln, rows_c, :]], axis=0).astype(BF16)
                v = jnp.concatenate([vr_ref[pln, rows_p, :], vr_ref[pln, rows_c, :]], axis=0).astype(BF16)
                o_acc = jnp.zeros((qb, LANES), F32)
                lse_acc = jnp.zeros((qb, LANES), F32)
                for h in range(HEADS_PER_PLANE):
                    qh = jnp.where(head_masks[h], q, 0.0).astype(BF16)
                    s = jnp.where(mask, _dot_nt(qh, k), -jnp.inf)
                    m = jnp.max(s, axis=-1, keepdims=True)
                    pexp = jnp.exp(s - m)
                    l = jnp.sum(pexp, axis=-1, keepdims=True)
                    oh = _dot(pexp.astype(BF16), v)
                    o_acc = jnp.where(head_masks[h], oh / l, o_acc)
                    lse_acc = jnp.where(head_masks[h], m + jnp.log(l), lse_acc)
                op_ref[p, pln, rows_c, :] = o_acc
                lse_ref[p, pln, rows_c, :] = lse_acc
            return carry

        lax.fori_loop(0, dil * nb, block, 0)

    def combine(i, carry):
        rows = pl.ds(pl.multiple_of(i * row_chunk, row_chunk), row_chunk)
        for pln in range(ATTN_PLANES):
            lses = [lse_ref[p, pln, rows, :] for p in range(len(DILATED_PATTERNS))]
            m = functools.reduce(jnp.maximum, lses)
            es = [jnp.exp(x - m) for x in lses]
            num = sum(e * op_ref[p, pln, rows, :] for p, e in enumerate(es))
            o_ref[0, rows, pln * LANES:(pln + 1) * LANES] = (num / sum(es)).astype(o_ref.dtype)
        return carry

    lax.fori_loop(0, seq // row_chunk, combine, 0)


def dilated_attention(proj, cos_t, sa_t, sb_t):
    b, s, _ = proj.shape
    assert all(s % (QUERY_BLOCK * d) == 0 and w // d <= QUERY_BLOCK for w, d in DILATED_PATTERNS)
    n_pat = len(DILATED_PATTERNS)
    tab = pl.BlockSpec((1, s, LANES), lambda i: (i, 0, 0))
    plane = pltpu.VMEM((ATTN_PLANES, s, LANES), F32)
    stats = pltpu.VMEM((n_pat, ATTN_PLANES, s, LANES), F32)
    return pl.pallas_call(
        functools.partial(_attn_kernel, row_chunk=256),
        grid=(b,),
        in_specs=[pl.BlockSpec((1, s, 3 * ATTN_WIDTH), lambda i: (i, 0, 0)), tab, tab, tab],
        out_specs=pl.BlockSpec((1, s, ATTN_WIDTH), lambda i: (i, 0, 0)),
        out_shape=jax.ShapeDtypeStruct((b, s, ATTN_WIDTH), BF16),
        scratch_shapes=[plane, plane, plane, stats, stats],
        compiler_params=_params(1),
        name="dilated_attention",
    )(proj, cos_t, sa_t, sb_t)


def _sconv_kernel(x_ref, w_ref, o_ref, win_ref, *, row_chunk):
    seq = x_ref.shape[1]
    cw = CONV_WIDTH

    def body(i, carry):
        start = pl.multiple_of(i * row_chunk, row_chunk)
        pstart = pl.multiple_of(jnp.maximum(start - HALO, 0), HALO)
        rows, prows = pl.ds(start, row_chunk), pl.ds(pstart, HALO)
        zp = x_ref[0, prows, cw:2 * cw] * x_ref[0, prows, 2 * cw:3 * cw]
        win_ref[0:HALO, :] = jnp.where(i > 0, zp, 0.0)
        win_ref[HALO:HALO + row_chunk, :] = x_ref[0, rows, cw:2 * cw] * x_ref[0, rows, 2 * cw:3 * cw]
        y = sum(w_ref[j:j + 1, :] * win_ref[pl.ds(HALO - (CONV_K - 1) + j, row_chunk), :]
                for j in range(CONV_K))
        o_ref[0, rows, :] = (x_ref[0, rows, 0:cw] * y).astype(o_ref.dtype)
        return carry

    lax.fori_loop(0, seq // row_chunk, body, 0)


def short_conv(proj, conv_w, row_chunk=256):
    b, s, _ = proj.shape
    return pl.pallas_call(
        functools.partial(_sconv_kernel, row_chunk=row_chunk),
        grid=(b,),
        in_specs=[pl.BlockSpec((1, s, 3 * CONV_WIDTH), lambda i: (i, 0, 1)), _resident((CONV_K, CONV_WIDTH))],
        out_specs=pl.BlockSpec((1, s, CONV_WIDTH), lambda i: (i, 0, 0)),
        out_shape=jax.ShapeDtypeStruct((b, s, CONV_WIDTH), BF16),
        scratch_shapes=[pltpu.VMEM((HALO + row_chunk, CONV_WIDTH), F32)],
        compiler_params=_params(1),
        name="short_conv",
    )(proj, conv_w)


def _mm(a16, b16):
    return jnp.dot(a16, b16, preferred_element_type=F32)


def _unit_lower_inverse_minus_eye(a_list, diag_blocks):
    c = a_list[0].shape[0]
    assert c // GDN_SUB == 4
    ds = [jnp.where(diag_blocks, a, 0.0) for a in a_list]
    es = [a - d for a, d in zip(a_list, ds)]
    ns = [-d for d in ds]
    d16 = [d.astype(BF16) for d in ds]
    ps = [_mm(d, d) for d in d16]
    n_factors = GDN_SUB.bit_length() - 2
    for f in range(n_factors):
        p16 = [p.astype(BF16) for p in ps]
        if f + 1 < n_factors:
            both = [_mm(jnp.concatenate([n.astype(BF16), p], axis=0), p) for n, p in zip(ns, p16)]
            ns = [n + p + b[:c] for n, p, b in zip(ns, ps, both)]
            ps = [b[c:] for b in both]
        else:
            ns = [n + p + _mm(n.astype(BF16), q) for n, p, q in zip(ns, ps, p16)]
    e16 = [e.astype(BF16) for e in es]
    ms = [e + _mm(n.astype(BF16), q) for e, n, q in zip(es, ns, e16)]
    m16 = [m.astype(BF16) for m in ms]
    m2s = [_mm(m, m) for m in m16]
    qs = [m2 - m - _mm(q, m2.astype(BF16)) for m, m2, q in zip(ms, m2s, m16)]
    return [q + n + _mm(q.astype(BF16), n.astype(BF16)) for q, n in zip(qs, ns)]


def _gdn_kernel(qkv_ref, gate_ref, ab_ref, cw_ref, alog_ref, dtb_ref, nw_ref, o_ref, win_ref, state_ref, *, group):
    seq = qkv_ref.shape[1]
    c, dh, nh = GDN_CHUNK, GDN_HEAD_DIM, GDN_HEADS
    rows_g = group * c
    row = lax.broadcasted_iota(jnp.int32, (c, c), 0)
    col = lax.broadcasted_iota(jnp.int32, (c, c), 1)
    causal = row >= col
    strict = row > col
    diag_blocks = (row // GDN_SUB) == (col // GDN_SUB)
    grow = lax.broadcasted_iota(jnp.int32, (rows_g, rows_g), 0)
    gcol = lax.broadcasted_iota(jnp.int32, (rows_g, rows_g), 1)
    tril_group = ((grow >= gcol) & (grow // c == gcol // c)).astype(BF16)
    state_ref[...] = jnp.zeros_like(state_ref)
    units = [(j, h) for j in range(group) for h in range(nh)]

    def body(gi, carry):
        base = pl.multiple_of(gi * rows_g, rows_g)
        pbase = pl.multiple_of(jnp.maximum(base - HALO, 0), HALO)
        rows = pl.ds(base, rows_g)
        win_ref[0:HALO, :] = jnp.where(gi > 0, qkv_ref[0, pl.ds(pbase, HALO), :], 0.0)
        win_ref[HALO:HALO + rows_g, :] = qkv_ref[0, rows, :]
        xc = sum(cw_ref[j:j + 1, :] * win_ref[pl.ds(HALO - (GDN_CONV_K - 1) + j, rows_g), :]
                 for j in range(GDN_CONV_K))
        xc = _silu(xc)

        ab = ab_ref[0, rows, :]
        z = ab + dtb_ref[...]
        softplus = jnp.maximum(z, 0.0) + jnp.log1p(jnp.exp(-jnp.abs(z)))
        g_all = -jnp.exp(alog_ref[...]) * softplus
        beta_all = 1.0 / (1.0 + jnp.exp(-ab))
        g_hi = g_all.astype(BF16)
        g_r = g_all - g_hi.astype(F32)
        g_mid = g_r.astype(BF16)
        g_lo = (g_r - g_mid.astype(F32)).astype(BF16)
        decay_all = _mm(tril_group, g_hi) + _mm(tril_group, g_mid) + _mm(tril_group, g_lo)
        decay_t = decay_all.T

        qs, ks, vbs, kbs, rels, dcols, edecs, dlasts = [], [], [], [], [], [], [], []
        for j, h in units:
            r0 = j * c
            q = xc[r0:r0 + c, h * dh:(h + 1) * dh]
            k = xc[r0:r0 + c, GDN_WIDTH + h * dh:GDN_WIDTH + (h + 1) * dh]
            v = xc[r0:r0 + c, 2 * GDN_WIDTH + h * dh:2 * GDN_WIDTH + (h + 1) * dh]
            q = q * lax.rsqrt(jnp.sum(q * q, axis=-1, keepdims=True) + EPS) * (dh ** -0.5)
            k = k * lax.rsqrt(jnp.sum(k * k, axis=-1, keepdims=True) + EPS)
            dcol = decay_all[r0:r0 + c, h:h + 1]
            drow = decay_t[h:h + 1, r0:r0 + c]
            beta = beta_all[r0:r0 + c, nh + h:nh + h + 1]
            qs.append(q)
            ks.append(k)
            vbs.append(v * beta)
            kbs.append(k * beta)
            rels.append(jnp.exp(jnp.where(causal, dcol - drow, -jnp.inf)))
            dcols.append(dcol)
            edecs.append(jnp.exp(dcol))
            dlasts.append(decay_all[r0 + c - 1:r0 + c, h:h + 1])

        k16 = [k.astype(BF16) for k in ks]
        kq = [_dot_nt(jnp.concatenate([kb, q], axis=0).astype(BF16), k) for kb, q, k in zip(kbs, qs, k16)]
        a_list = [jnp.where(strict, x[:c] * rel, 0.0) for x, rel in zip(kq, rels)]
        attn16 = [jnp.where(causal, x[c:] * rel, 0.0).astype(BF16) for x, rel in zip(kq, rels)]
        n_list = _unit_lower_inverse_minus_eye(a_list, diag_blocks)
        rhs = [jnp.concatenate([vb, kb * ed], axis=1) for vb, kb, ed in zip(vbs, kbs, edecs)]
        sol = [r + _mm(n.astype(BF16), r.astype(BF16)) for r, n in zip(rhs, n_list)]
        wq16 = [jnp.concatenate([x[:, dh:], q * ed], axis=0).astype(BF16) for x, q, ed in zip(sol, qs, edecs)]
        ak16 = [jnp.concatenate([at, (k * jnp.exp(dl - dc)).T.astype(BF16)], axis=0)
                for at, k, dl, dc in zip(attn16, ks, dlasts, dcols)]

        states = [state_ref[h] for h in range(nh)]
        for j in range(group):
            idx = [j * nh + h for h in range(nh)]
            ws_qs = [_mm(wq16[i], states[h].astype(BF16)) for h, i in enumerate(idx)]
            v16 = [(sol[i][:, :dh] - x[:c]).astype(BF16) for i, x in zip(idx, ws_qs)]
            av_kv = [_mm(ak16[i], v) for i, v in zip(idx, v16)]
            states = [s * jnp.exp(dlasts[i]) + y[c:] for s, i, y in zip(states, idx, av_kv)]
            out_rows = pl.ds(base + j * c, c)
            for h in range(nh):
                o = ws_qs[h][c:] + av_kv[h][:c]
                gate = gate_ref[0, out_rows, h * dh:(h + 1) * dh]
                o_ref[0, out_rows, h * dh:(h + 1) * dh] = (_rms(o, nw_ref[...]) * _silu(gate)).astype(o_ref.dtype)
        for h in range(nh):
            state_ref[h] = states[h]
        return carry

    lax.fori_loop(0, seq // rows_g, body, 0)


def gated_deltanet(proj, conv_w, a_log, dt_bias, norm_w, group=2):
    b, s, _ = proj.shape
    pad = jnp.zeros((LANES - GDN_HEADS,), F32)
    alog_v = jnp.concatenate([a_log.astype(F32), pad]).reshape(1, LANES)
    dtb_v = jnp.concatenate([dt_bias.astype(F32), pad]).reshape(1, LANES)
    qkv_w = 3 * GDN_WIDTH
    return pl.pallas_call(
        functools.partial(_gdn_kernel, group=group),
        grid=(b,),
        in_specs=[pl.BlockSpec((1, s, qkv_w), lambda i: (i, 0, 1)),
                  pl.BlockSpec((1, s, GDN_WIDTH), lambda i: (i, 0, (qkv_w * 2) // GDN_WIDTH)),
                  pl.BlockSpec((1, s, LANES), lambda i: (i, 0, IN_MAIN // LANES)),
                  _resident((GDN_CONV_K, qkv_w)), _resident((1, LANES)), _resident((1, LANES)),
                  _resident((1, GDN_HEAD_DIM))],
        out_specs=pl.BlockSpec((1, s, GDN_WIDTH), lambda i: (i, 0, 0)),
        out_shape=jax.ShapeDtypeStruct((b, s, GDN_WIDTH), BF16),
        scratch_shapes=[pltpu.VMEM((HALO + group * GDN_CHUNK, qkv_w), F32),
                        pltpu.VMEM((GDN_HEADS, GDN_HEAD_DIM, GDN_HEAD_DIM), F32)],
        compiler_params=_params(1),
        name="gated_deltanet",
    )(proj, proj, proj, conv_w, alog_v, dtb_v, norm_w.reshape(1, GDN_HEAD_DIM))


def _out_proj_kernel(ya_ref, yc_ref, yg_ref, w_ref, g_ref, h_ref, o_ref):
    a_w, c_w = ya_ref.shape[1], yc_ref.shape[1]
    mix = (_dot(ya_ref[...], w_ref[0:a_w, :]) + _dot(yc_ref[...], w_ref[a_w:a_w + c_w, :])
           + _dot(yg_ref[...], w_ref[a_w + c_w:, :]))
    o_ref[...] = h_ref[...] + _rms(mix, g_ref[...])


def out_proj(ya, yc, yg, w, gain, h, row_tile=512):
    t, d = h.shape
    rows = lambda width: pl.BlockSpec((row_tile, width), lambda i: (i, 0))
    return pl.pallas_call(
        _out_proj_kernel,
        grid=(t // row_tile,),
        in_specs=[rows(ya.shape[1]), rows(yc.shape[1]), rows(yg.shape[1]), _resident(w.shape),
                  _resident((1, d)), rows(d)],
        out_specs=rows(d),
        out_shape=jax.ShapeDtypeStruct((t, d), F32),
        compiler_params=_params(1),
        name="out_proj",
    )(ya, yc, yg, w, gain.reshape(1, d), h)


def _xattn_kernel(h_ref, kv_ref, wq_ref, wo_ref, gpre_ref, gpost_ref, o_ref):
    h = h_ref[0]
    d = h.shape[-1]
    dh = d // XATTN_HEADS
    q = _dot(_rms(h, gpre_ref[...]).astype(BF16), wq_ref[...]).astype(BF16)
    heads = []
    for i in range(XATTN_HEADS):
        k = kv_ref[0, :, i * dh:(i + 1) * dh]
        v = kv_ref[0, :, d + i * dh:d + (i + 1) * dh]
        s = _dot_nt(q[:, i * dh:(i + 1) * dh], k) * (dh ** -0.5)
        p = jnp.exp(s - jnp.max(s, axis=-1, keepdims=True))
        l = jnp.sum(p, axis=-1, keepdims=True)
        heads.append((_dot(p.astype(BF16), v) / l).astype(BF16))
    xa = _dot(jnp.concatenate(heads, axis=-1), wo_ref[...])
    o_ref[0] = h + _rms(xa, gpost_ref[...])


def cross_attention(h, kv, wq, wo, g_pre, g_post, row_tile=512):
    b, s, d = h.shape
    m = kv.shape[1]
    return pl.pallas_call(
        _xattn_kernel,
        grid=(b, s // row_tile),
        in_specs=[pl.BlockSpec((1, row_tile, d), lambda i, j: (i, j, 0)),
                  pl.BlockSpec((1, m, 2 * d), lambda i, j: (i, 0, 0)),
                  _resident((d, d)), _resident((d, d)), _resident((1, d)), _resident((1, d))],
        out_specs=pl.BlockSpec((1, row_tile, d), lambda i, j: (i, j, 0)),
        out_shape=jax.ShapeDtypeStruct((b, s, d), F32),
        compiler_params=_params(2),
        name="cross_attention",
    )(h, kv, wq, wo, g_pre.reshape(1, d), g_post.reshape(1, d))


def _ffn_kernel(h_ref, wgu_ref, wd_ref, gpre_ref, gpost_ref, o_ref, acc_ref, *, hid_chunk):
    h = h_ref[...]
    hidden = wd_ref.shape[0]
    hn = _rms(h, gpre_ref[...]).astype(BF16)
    for idx, c in enumerate(range(0, hidden, hid_chunk)):
        e = min(c + hid_chunk, hidden)
        gate = _dot(hn, wgu_ref[:, c:e])
        up = _dot(hn, wgu_ref[:, hidden + c:hidden + e])
        part = _dot((_silu(gate) * up).astype(BF16), wd_ref[c:e, :])
        if idx == 0:
            acc_ref[...] = part
        else:
            acc_ref[...] += part
    o_ref[...] = h + _rms(acc_ref[...], gpost_ref[...])


def swiglu_ffn(h, wgu, wd, g_pre, g_post, row_tile=512, hid_chunk=512):
    t, d = h.shape
    rows = pl.BlockSpec((row_tile, d), lambda i: (i, 0))
    return pl.pallas_call(
        functools.partial(_ffn_kernel, hid_chunk=hid_chunk),
        grid=(t // row_tile,),
        in_specs=[rows, _resident(wgu.shape), _resident(wd.shape), _resident((1, d)), _resident((1, d))],
        out_specs=rows,
        out_shape=jax.ShapeDtypeStruct((t, d), F32),
        scratch_shapes=[pltpu.VMEM((row_tile, d), F32)],
        compiler_params=_params(1),
        name="swiglu_ffn",
    )(h, wgu, wd, g_pre.reshape(1, d), g_post.reshape(1, d))


def _rotary_tables(positions):
    half = ROPE_DIM // 2
    inv_freq = jnp.float32(ROPE_THETA) ** (-jnp.arange(0, ROPE_DIM, 2, dtype=F32) / ROPE_DIM)
    ang = positions.astype(F32)[..., None] * inv_freq
    cos, sin = jnp.cos(ang), jnp.sin(ang)
    rest = ATTN_HEAD_DIM - ROPE_DIM
    ones = jnp.ones(ang.shape[:-1] + (rest,), F32)
    zeros = jnp.zeros(ang.shape[:-1] + (rest,), F32)
    zh = jnp.zeros_like(sin)
    reps = LANES // ATTN_HEAD_DIM
    tile = lambda parts: jnp.tile(jnp.concatenate(parts, axis=-1), (1, 1, reps))
    return tile([cos, cos, ones]), tile([-sin, zh, zeros]), tile([zh, sin, zeros])


def _arrange_w_in(w):
    ab0 = IN_MAIN - GDN_WIDTH
    ab1 = ab0 + 2 * GDN_HEADS
    pad = jnp.zeros((w.shape[0], LANES - 2 * GDN_HEADS), w.dtype)
    return jnp.concatenate([w[:, :ab0], w[:, ab1:], w[:, ab0:ab1], pad], axis=1).astype(BF16)


def kernel(x, mem, positions, norm_mix_pre, norm_mix_post, w_in, conv_short, conv_gdn, gdn_a_log, gdn_dt_bias, gdn_norm, w_out, norm_mem, norm_xattn_pre, norm_xattn_post, w_xq, w_xkv, w_xo, norm_ffn_pre, norm_ffn_post, w_gate_up, w_down):
    b, s, d = x.shape
    m = mem.shape[1]
    depth = w_in.shape[0]
    cos_t, sa_t, sb_t = _rotary_tables(positions)
    mem2 = mem.reshape(b * m, d)
    h = x.reshape(b * s, d)
    for l in range(depth):
        w_in_l = _arrange_w_in(w_in[l])
        proj = norm_matmul(h, norm_mix_pre[l], w_in_l, out_dtype=F32).reshape(b, s, IN_PAD)
        ya = dilated_attention(proj, cos_t, sa_t, sb_t)
        yc = short_conv(proj, conv_short[l])
        yg = gated_deltanet(proj, conv_gdn[l], gdn_a_log[l], gdn_dt_bias[l], gdn_norm[l])
        h = out_proj(ya.reshape(b * s, -1), yc.reshape(b * s, -1), yg.reshape(b * s, -1),
                     w_out[l].astype(BF16), norm_mix_post[l], h)
        kv = norm_matmul(mem2, norm_mem[l], w_xkv[l].astype(BF16), out_dtype=BF16).reshape(b, m, 2 * d)
        h = cross_attention(h.reshape(b, s, d), kv, w_xq[l].astype(BF16), w_xo[l].astype(BF16),
                            norm_xattn_pre[l], norm_xattn_post[l]).reshape(b * s, d)
        h = swiglu_ffn(h, w_gate_up[l].astype(BF16), w_down[l].astype(BF16),
                       norm_ffn_pre[l], norm_ffn_post[l])
    return h.reshape(b, s, d)
```

```python
import functools

import jax
import jax.numpy as jnp
from jax import lax
from jax.experimental import pallas as pl
from jax.experimental.pallas import tpu as pltpu

F32 = jnp.float32
BF16 = jnp.bfloat16
EPS = 1e-6

ATTN_HEADS = 4
ATTN_HEAD_DIM = 64
ATTN_WIDTH = ATTN_HEADS * ATTN_HEAD_DIM
DILATED_PATTERNS = ((128, 1), (512, 4), (2048, 16))
QUERY_BLOCK = 128
ROPE_THETA = 500000.0
ROPE_DIM = ATTN_HEAD_DIM // 4
CONV_WIDTH = 256
CONV_K = 3
GDN_HEADS = 4
GDN_HEAD_DIM = 128
GDN_WIDTH = GDN_HEADS * GDN_HEAD_DIM
GDN_CONV_K = 4
GDN_CHUNK = 64
GDN_SUB = 16
XATTN_HEADS = 4
LANES = 128
ATTN_PLANES = ATTN_WIDTH // LANES
HEADS_PER_PLANE = LANES // ATTN_HEAD_DIM
HALO = 8
IN_MAIN = 3 * ATTN_WIDTH + 3 * CONV_WIDTH + 4 * GDN_WIDTH
IN_PAD = IN_MAIN + LANES
VMEM_LIMIT = 56 * 1024 * 1024


def _params(n_grid):
    return pltpu.CompilerParams(dimension_semantics=("arbitrary",) * n_grid,
                                vmem_limit_bytes=VMEM_LIMIT)


def _rms(x, gain):
    return x * lax.rsqrt(jnp.mean(x * x, axis=-1, keepdims=True) + EPS) * gain


def _dot(a, b):
    return jnp.dot(a, b, preferred_element_type=F32)


def _dot_nt(a, b):
    return lax.dot_general(a, b, (((1,), (1,)), ((), ())), preferred_element_type=F32)


def _silu(x):
    return x * (1.0 / (1.0 + jnp.exp(-x)))


def _layer_block(stacked, layer):
    tail = stacked.shape[1:]
    return pl.BlockSpec((None,) + tail, lambda *_: (layer,) + (0,) * len(tail))


def _norm_matmul_kernel(x_ref, g_ref, w_ref, o_ref, *, col_chunk):
    xn = _rms(x_ref[...], g_ref[...]).astype(BF16)
    n = w_ref.shape[1]
    for c in range(0, n, col_chunk):
        e = min(c + col_chunk, n)
        o_ref[:, c:e] = _dot(xn, w_ref[:, c:e]).astype(o_ref.dtype)


def norm_matmul(x, gain, w, layer, *, out_dtype, row_tile=512, col_chunk=512):
    t, d = x.shape
    n = w.shape[2]
    return pl.pallas_call(
        functools.partial(_norm_matmul_kernel, col_chunk=col_chunk),
        grid=(t // row_tile,),
        in_specs=[pl.BlockSpec((row_tile, d), lambda i: (i, 0)), _layer_block(gain, layer), _layer_block(w, layer)],
        out_specs=pl.BlockSpec((row_tile, n), lambda i: (i, 0)),
        out_shape=jax.ShapeDtypeStruct((t, n), out_dtype),
        compiler_params=_params(1),
        name="norm_matmul",
    )(x, gain, w)


def _attn_kernel(qkv_ref, cos_ref, sa_ref, sb_ref, o_ref, qr_ref, kr_ref, vr_ref, op_ref, lse_ref, *,
                 row_chunk, blocks_per_iter):
    assert HEADS_PER_PLANE == 2
    seq = qkv_ref.shape[1]
    qb = QUERY_BLOCK

    def rotate(i, carry):
        rows = pl.ds(pl.multiple_of(i * row_chunk, row_chunk), row_chunk)
        c, sa, sb = cos_ref[0, rows, :], sa_ref[0, rows, :], sb_ref[0, rows, :]
        for pln in range(ATTN_PLANES):
            for src, dst, scale in ((0, qr_ref, ATTN_HEAD_DIM ** -0.5), (ATTN_WIDTH, kr_ref, 1.0)):
                x = qkv_ref[0, rows, src + pln * LANES:src + (pln + 1) * LANES]
                rot = (x * c + pltpu.roll(x, LANES - ROPE_DIM // 2, 1) * sa
                       + pltpu.roll(x, ROPE_DIM // 2, 1) * sb)
                dst[pln, rows, :] = rot * scale
            vr_ref[pln, rows, :] = qkv_ref[0, rows, 2 * ATTN_WIDTH + pln * LANES:2 * ATTN_WIDTH + (pln + 1) * LANES]
        return carry

    lax.fori_loop(0, seq // row_chunk, rotate, 0)

    lane = lax.broadcasted_iota(jnp.int32, (1, LANES), 1)
    head_masks = [(lane >= ATTN_HEAD_DIM * h) & (lane < ATTN_HEAD_DIM * (h + 1)) for h in range(HEADS_PER_PLANE)]
    qi = lax.broadcasted_iota(jnp.int32, (qb, 2 * qb), 0)
    kj = lax.broadcasted_iota(jnp.int32, (qb, 2 * qb), 1)
    dist = qi + qb - kj

    for p, (window, dil) in enumerate(DILATED_PATTERNS):
        n_back = window // dil
        nb = seq // dil // qb
        band = (dist >= 0) & (dist <= n_back)

        def blocks(it, carry, p=p, dil=dil, nb=nb, band=band):
            dests, masks, qs, ks, vs = [], [], [], [], []
            for u in range(blocks_per_iter):
                blk = it * blocks_per_iter + u
                r = blk // nb
                n = blk % nb
                cur = r + n * (qb * dil)
                prv = r + jnp.maximum(n - 1, 0) * (qb * dil)
                if dil == 1:
                    rows_c, rows_p = pl.ds(pl.multiple_of(cur, qb), qb), pl.ds(pl.multiple_of(prv, qb), qb)
                else:
                    rows_c, rows_p = pl.ds(cur, qb, stride=dil), pl.ds(prv, qb, stride=dil)
                mask = band & ((kj >= qb) | (n > 0))
                for pln in range(ATTN_PLANES):
                    dests.append((pln, rows_c))
                    masks.append(mask)
                    qs.append(qr_ref[pln, rows_c, :])
                    ks.append(jnp.concatenate([kr_ref[pln, rows_p, :], kr_ref[pln, rows_c, :]], axis=0).astype(BF16))
                    vs.append(jnp.concatenate([vr_ref[pln, rows_p, :], vr_ref[pln, rows_c, :]], axis=0).astype(BF16))
            heads = range(HEADS_PER_PLANE)
            qh = [[jnp.where(head_masks[h], q, 0.0).astype(BF16) for h in heads] for q in qs]
            s = [[jnp.where(mask, _dot_nt(x, k), -jnp.inf) for x in xs] for xs, k, mask in zip(qh, ks, masks)]
            m = [[jnp.max(x, axis=-1, keepdims=True) for x in xs] for xs in s]
            pexp = [[jnp.exp(x - y) for x, y in zip(xs, ys)] for xs, ys in zip(s, m)]
            l = [[jnp.sum(x, axis=-1, keepdims=True) for x in xs] for xs in pexp]
            oh = [[_dot(x.astype(BF16), v) for x in xs] for xs, v in zip(pexp, vs)]
            for (pln, rows_c), os, ls, ms in zip(dests, oh, l, m):
                op_ref[p, pln, rows_c, :] = jnp.where(head_masks[0], os[0] / ls[0], os[1] / ls[1])
                lse_ref[p, pln, rows_c, :] = jnp.where(head_masks[0], ms[0] + jnp.log(ls[0]), ms[1] + jnp.log(ls[1]))
            return carry

        lax.fori_loop(0, dil * nb // blocks_per_iter, blocks, 0)

    def combine(i, carry):
        rows = pl.ds(pl.multiple_of(i * row_chunk, row_chunk), row_chunk)
        for pln in range(ATTN_PLANES):
            lses = [lse_ref[p, pln, rows, :] for p in range(len(DILATED_PATTERNS))]
            m = functools.reduce(jnp.maximum, lses)
            es = [jnp.exp(x - m) for x in lses]
            num = sum(e * op_ref[p, pln, rows, :] for p, e in enumerate(es))
            o_ref[0, rows, pln * LANES:(pln + 1) * LANES] = (num / sum(es)).astype(o_ref.dtype)
        return carry

    lax.fori_loop(0, seq // row_chunk, combine, 0)


def dilated_attention(proj, cos_t, sa_t, sb_t):
    b, s, _ = proj.shape
    assert all(s % (QUERY_BLOCK * d) == 0 and w // d <= QUERY_BLOCK for w, d in DILATED_PATTERNS)
    n_pat = len(DILATED_PATTERNS)
    tab = pl.BlockSpec((1, s, LANES), lambda i: (i, 0, 0))
    plane = pltpu.VMEM((ATTN_PLANES, s, LANES), F32)
    stats = pltpu.VMEM((n_pat, ATTN_PLANES, s, LANES), F32)
    return pl.pallas_call(
        functools.partial(_attn_kernel, row_chunk=256, blocks_per_iter=2),
        grid=(b,),
        in_specs=[pl.BlockSpec((1, s, 3 * ATTN_WIDTH), lambda i: (i, 0, 0)), tab, tab, tab],
        out_specs=pl.BlockSpec((1, s, ATTN_WIDTH), lambda i: (i, 0, 0)),
        out_shape=jax.ShapeDtypeStruct((b, s, ATTN_WIDTH), BF16),
        scratch_shapes=[plane, plane, plane, stats, stats],
        compiler_params=_params(1),
        name="dilated_attention",
    )(proj, cos_t, sa_t, sb_t)


def _sconv_kernel(x_ref, w_ref, o_ref, win_ref, *, row_chunk):
    seq = x_ref.shape[1]
    cw = CONV_WIDTH

    def body(i, carry):
        start = pl.multiple_of(i * row_chunk, row_chunk)
        pstart = pl.multiple_of(jnp.maximum(start - HALO, 0), HALO)
        rows, prows = pl.ds(start, row_chunk), pl.ds(pstart, HALO)
        zp = x_ref[0, prows, cw:2 * cw] * x_ref[0, prows, 2 * cw:3 * cw]
        win_ref[0:HALO, :] = jnp.where(i > 0, zp, 0.0)
        win_ref[HALO:HALO + row_chunk, :] = x_ref[0, rows, cw:2 * cw] * x_ref[0, rows, 2 * cw:3 * cw]
        y = sum(w_ref[j:j + 1, :] * win_ref[pl.ds(HALO - (CONV_K - 1) + j, row_chunk), :]
                for j in range(CONV_K))
        o_ref[0, rows, :] = (x_ref[0, rows, 0:cw] * y).astype(o_ref.dtype)
        return carry

    lax.fori_loop(0, seq // row_chunk, body, 0)


def short_conv(proj, conv_w, layer, row_chunk=256):
    b, s, _ = proj.shape
    return pl.pallas_call(
        functools.partial(_sconv_kernel, row_chunk=row_chunk),
        grid=(b,),
        in_specs=[pl.BlockSpec((1, s, 3 * CONV_WIDTH), lambda i: (i, 0, 1)), _layer_block(conv_w, layer)],
        out_specs=pl.BlockSpec((1, s, CONV_WIDTH), lambda i: (i, 0, 0)),
        out_shape=jax.ShapeDtypeStruct((b, s, CONV_WIDTH), BF16),
        scratch_shapes=[pltpu.VMEM((HALO + row_chunk, CONV_WIDTH), F32)],
        compiler_params=_params(1),
        name="short_conv",
    )(proj, conv_w)


def _mm(a16, b16):
    return jnp.dot(a16, b16, preferred_element_type=F32)


def _lane_sum(x, ones16):
    hi = x.astype(BF16)
    lo = (x - hi.astype(F32)).astype(BF16)
    return _mm(hi, ones16) + _mm(lo, ones16)


def _unit_lower_inverse_minus_eye(a_list, diag_blocks):
    c = a_list[0].shape[0]
    assert c // GDN_SUB == 4
    ds = [jnp.where(diag_blocks, a, 0.0) for a in a_list]
    es = [a - d for a, d in zip(a_list, ds)]
    ns = [-d for d in ds]
    d16 = [d.astype(BF16) for d in ds]
    ps = [_mm(d, d) for d in d16]
    n_factors = GDN_SUB.bit_length() - 2
    for f in range(n_factors):
        p16 = [p.astype(BF16) for p in ps]
        if f + 1 < n_factors:
            both = [_mm(jnp.concatenate([n.astype(BF16), p], axis=0), p) for n, p in zip(ns, p16)]
            ns = [n + p + b[:c] for n, p, b in zip(ns, ps, both)]
            ps = [b[c:] for b in both]
        else:
            ns = [n + p + _mm(n.astype(BF16), q) for n, p, q in zip(ns, ps, p16)]
    e16 = [e.astype(BF16) for e in es]
    ms = [e + _mm(n.astype(BF16), q) for e, n, q in zip(es, ns, e16)]
    m16 = [m.astype(BF16) for m in ms]
    m2s = [_mm(m, m) for m in m16]
    qs = [m2 - m - _mm(q, m2.astype(BF16)) for m, m2, q in zip(ms, m2s, m16)]
    return [q + n + _mm(q.astype(BF16), n.astype(BF16)) for q, n in zip(qs, ns)]


def _gdn_kernel(qkv_ref, gate_ref, ab_ref, cw_ref, alog_ref, dtb_ref, nw_ref, o_ref, win_ref, state_ref, *, group):
    seq = qkv_ref.shape[1]
    c, dh, nh = GDN_CHUNK, GDN_HEAD_DIM, GDN_HEADS
    rows_g = group * c
    row = lax.broadcasted_iota(jnp.int32, (c, c), 0)
    col = lax.broadcasted_iota(jnp.int32, (c, c), 1)
    causal = row >= col
    strict = row > col
    diag_blocks = (row // GDN_SUB) == (col // GDN_SUB)
    grow = lax.broadcasted_iota(jnp.int32, (rows_g, rows_g), 0)
    gcol = lax.broadcasted_iota(jnp.int32, (rows_g, rows_g), 1)
    tril_group = ((grow >= gcol) & (grow // c == gcol // c)).astype(BF16)
    ones16 = jnp.ones((dh, dh), BF16)
    state_ref[...] = jnp.zeros_like(state_ref)
    units = [(j, h) for j in range(group) for h in range(nh)]

    def body(gi, carry):
        base = pl.multiple_of(gi * rows_g, rows_g)
        pbase = pl.multiple_of(jnp.maximum(base - HALO, 0), HALO)
        rows = pl.ds(base, rows_g)
        win_ref[0:HALO, :] = jnp.where(gi > 0, qkv_ref[0, pl.ds(pbase, HALO), :], 0.0)
        win_ref[HALO:HALO + rows_g, :] = qkv_ref[0, rows, :]
        xc = sum(cw_ref[j:j + 1, :] * win_ref[pl.ds(HALO - (GDN_CONV_K - 1) + j, rows_g), :]
                 for j in range(GDN_CONV_K))
        xc = _silu(xc)

        def l2_normalised(tile, scale):
            x = xc[:, tile * dh:(tile + 1) * dh]
            return x * (lax.rsqrt(_lane_sum(x * x, ones16) + EPS) * scale)

        qn = [l2_normalised(h, dh ** -0.5) for h in range(nh)]
        kn = [l2_normalised(nh + h, 1.0) for h in range(nh)]

        ab = ab_ref[0, rows, :]
        z = ab + dtb_ref[...]
        softplus = jnp.maximum(z, 0.0) + jnp.log1p(jnp.exp(-jnp.abs(z)))
        g_all = -jnp.exp(alog_ref[...]) * softplus
        beta_all = 1.0 / (1.0 + jnp.exp(-ab))
        g_hi = g_all.astype(BF16)
        g_r = g_all - g_hi.astype(F32)
        g_mid = g_r.astype(BF16)
        g_lo = (g_r - g_mid.astype(F32)).astype(BF16)
        decay_all = _mm(tril_group, g_hi) + _mm(tril_group, g_mid) + _mm(tril_group, g_lo)
        decay_t = decay_all.T
        beta_t = beta_all.T

        def lane_broadcast(row_vec):
            return jnp.broadcast_to(row_vec, (dh, c)).T

        qs, ks, vbs, kbs, rels, dcols, edecs, dlasts = [], [], [], [], [], [], [], []
        for j, h in units:
            r0 = j * c
            q = qn[h][r0:r0 + c]
            k = kn[h][r0:r0 + c]
            v = xc[r0:r0 + c, 2 * GDN_WIDTH + h * dh:2 * GDN_WIDTH + (h + 1) * dh]
            drow = decay_t[h:h + 1, r0:r0 + c]
            dcol = lane_broadcast(drow)
            beta = lane_broadcast(beta_t[nh + h:nh + h + 1, r0:r0 + c])
            qs.append(q)
            ks.append(k)
            vbs.append(v * beta)
            kbs.append(k * beta)
            rels.append(jnp.exp(jnp.where(causal, dcol[:, :c] - drow, -jnp.inf)))
            dcols.append(dcol)
            edecs.append(jnp.exp(dcol))
            dlasts.append(dcol[c - 1:c, :])

        k16 = [k.astype(BF16) for k in ks]
        kq = [_dot_nt(jnp.concatenate([kb, q], axis=0).astype(BF16), k) for kb, q, k in zip(kbs, qs, k16)]
        a_list = [jnp.where(strict, x[:c] * rel, 0.0) for x, rel in zip(kq, rels)]
        attn16 = [jnp.where(causal, x[c:] * rel, 0.0).astype(BF16) for x, rel in zip(kq, rels)]
        n_list = _unit_lower_inverse_minus_eye(a_list, diag_blocks)
        rhs = [jnp.concatenate([vb, kb * ed], axis=1) for vb, kb, ed in zip(vbs, kbs, edecs)]
        sol = [r + _mm(n.astype(BF16), r.astype(BF16)) for r, n in zip(rhs, n_list)]
        wq16 = [jnp.concatenate([x[:, dh:], q * ed], axis=0).astype(BF16) for x, q, ed in zip(sol, qs, edecs)]
        ak16 = [jnp.concatenate([at, (k * jnp.exp(dl - dc)).T.astype(BF16)], axis=0)
                for at, k, dl, dc in zip(attn16, ks, dlasts, dcols)]

        states = [state_ref[h] for h in range(nh)]
        for j in range(group):
            idx = [j * nh + h for h in range(nh)]
            ws_qs = [_mm(wq16[i], states[h].astype(BF16)) for h, i in enumerate(idx)]
            v16 = [(sol[i][:, :dh] - x[:c]).astype(BF16) for i, x in zip(idx, ws_qs)]
            av_kv = [_mm(ak16[i], v) for i, v in zip(idx, v16)]
            states = [s * jnp.exp(dlasts[i]) + y[c:] for s, i, y in zip(states, idx, av_kv)]
            out_rows = pl.ds(base + j * c, c)
            for h in range(nh):
                o = ws_qs[h][c:] + av_kv[h][:c]
                gate = gate_ref[0, out_rows, h * dh:(h + 1) * dh]
                o_ref[0, out_rows, h * dh:(h + 1) * dh] = (_rms(o, nw_ref[...]) * _silu(gate)).astype(o_ref.dtype)
        for h in range(nh):
            state_ref[h] = states[h]
        return carry

    lax.fori_loop(0, seq // rows_g, body, 0)


def gated_deltanet(proj, conv_w, alog_v, dtb_v, norm_w, layer, group=4):
    b, s, _ = proj.shape
    qkv_w = 3 * GDN_WIDTH
    return pl.pallas_call(
        functools.partial(_gdn_kernel, group=group),
        grid=(b,),
        in_specs=[pl.BlockSpec((1, s, qkv_w), lambda i: (i, 0, 1)),
                  pl.BlockSpec((1, s, GDN_WIDTH), lambda i: (i, 0, (qkv_w * 2) // GDN_WIDTH)),
                  pl.BlockSpec((1, s, LANES), lambda i: (i, 0, IN_MAIN // LANES)),
                  _layer_block(conv_w, layer), _layer_block(alog_v, layer), _layer_block(dtb_v, layer),
                  _layer_block(norm_w, layer)],
        out_specs=pl.BlockSpec((1, s, GDN_WIDTH), lambda i: (i, 0, 0)),
        out_shape=jax.ShapeDtypeStruct((b, s, GDN_WIDTH), BF16),
        scratch_shapes=[pltpu.VMEM((HALO + group * GDN_CHUNK, qkv_w), F32),
                        pltpu.VMEM((GDN_HEADS, GDN_HEAD_DIM, GDN_HEAD_DIM), F32)],
        compiler_params=_params(1),
        name="gated_deltanet",
    )(proj, proj, proj, conv_w, alog_v, dtb_v, norm_w)


def _out_proj_kernel(ya_ref, yc_ref, yg_ref, w_ref, g_ref, h_ref, o_ref):
    a_w, c_w = ya_ref.shape[1], yc_ref.shape[1]
    mix = (_dot(ya_ref[...], w_ref[0:a_w, :]) + _dot(yc_ref[...], w_ref[a_w:a_w + c_w, :])
           + _dot(yg_ref[...], w_ref[a_w + c_w:, :]))
    o_ref[...] = h_ref[...] + _rms(mix, g_ref[...])


def out_proj(ya, yc, yg, w, gain, h, layer, row_tile=512):
    t, d = h.shape
    rows = lambda width: pl.BlockSpec((row_tile, width), lambda i: (i, 0))
    return pl.pallas_call(
        _out_proj_kernel,
        grid=(t // row_tile,),
        in_specs=[rows(ya.shape[1]), rows(yc.shape[1]), rows(yg.shape[1]), _layer_block(w, layer),
                  _layer_block(gain, layer), rows(d)],
        out_specs=rows(d),
        out_shape=jax.ShapeDtypeStruct((t, d), F32),
        compiler_params=_params(1),
        name="out_proj",
    )(ya, yc, yg, w, gain, h)


def _xattn_kernel(h_ref, kv_ref, wq_ref, wo_ref, gpre_ref, gpost_ref, o_ref):
    h = h_ref[0]
    d = h.shape[-1]
    dh = d // XATTN_HEADS
    q = _dot(_rms(h, gpre_ref[...]).astype(BF16), wq_ref[...]).astype(BF16)
    heads = []
    for i in range(XATTN_HEADS):
        k = kv_ref[0, :, i * dh:(i + 1) * dh]
        v = kv_ref[0, :, d + i * dh:d + (i + 1) * dh]
        s = _dot_nt(q[:, i * dh:(i + 1) * dh], k) * (dh ** -0.5)
        p = jnp.exp(s - jnp.max(s, axis=-1, keepdims=True))
        l = jnp.sum(p, axis=-1, keepdims=True)
        heads.append((_dot(p.astype(BF16), v) / l).astype(BF16))
    xa = _dot(jnp.concatenate(heads, axis=-1), wo_ref[...])
    o_ref[0] = h + _rms(xa, gpost_ref[...])


def cross_attention(h, kv, wq, wo, g_pre, g_post, layer, row_tile=512):
    b, s, d = h.shape
    m = kv.shape[1]
    return pl.pallas_call(
        _xattn_kernel,
        grid=(b, s // row_tile),
        in_specs=[pl.BlockSpec((1, row_tile, d), lambda i, j: (i, j, 0)),
                  pl.BlockSpec((1, m, 2 * d), lambda i, j: (i, 0, 0)),
                  _layer_block(wq, layer), _layer_block(wo, layer), _layer_block(g_pre, layer),
                  _layer_block(g_post, layer)],
        out_specs=pl.BlockSpec((1, row_tile, d), lambda i, j: (i, j, 0)),
        out_shape=jax.ShapeDtypeStruct((b, s, d), F32),
        compiler_params=_params(2),
        name="cross_attention",
    )(h, kv, wq, wo, g_pre, g_post)


def _ffn_kernel(h_ref, wgu_ref, wd_ref, gpre_ref, gpost_ref, o_ref, acc_ref, *, hid_chunk):
    h = h_ref[...]
    hidden = wd_ref.shape[0]
    hn = _rms(h, gpre_ref[...]).astype(BF16)
    for idx, c in enumerate(range(0, hidden, hid_chunk)):
        e = min(c + hid_chunk, hidden)
        gate = _dot(hn, wgu_ref[:, c:e])
        up = _dot(hn, wgu_ref[:, hidden + c:hidden + e])
        part = _dot((_silu(gate) * up).astype(BF16), wd_ref[c:e, :])
        if idx == 0:
            acc_ref[...] = part
        else:
            acc_ref[...] += part
    o_ref[...] = h + _rms(acc_ref[...], gpost_ref[...])


def swiglu_ffn(h, wgu, wd, g_pre, g_post, layer, row_tile=512, hid_chunk=512):
    t, d = h.shape
    rows = pl.BlockSpec((row_tile, d), lambda i: (i, 0))
    return pl.pallas_call(
        functools.partial(_ffn_kernel, hid_chunk=hid_chunk),
        grid=(t // row_tile,),
        in_specs=[rows, _layer_block(wgu, layer), _layer_block(wd, layer), _layer_block(g_pre, layer),
                  _layer_block(g_post, layer)],
        out_specs=rows,
        out_shape=jax.ShapeDtypeStruct((t, d), F32),
        scratch_shapes=[pltpu.VMEM((row_tile, d), F32)],
        compiler_params=_params(1),
        name="swiglu_ffn",
    )(h, wgu, wd, g_pre, g_post)


def _rotary_tables(positions):
    inv_freq = jnp.float32(ROPE_THETA) ** (-jnp.arange(0, ROPE_DIM, 2, dtype=F32) / ROPE_DIM)
    ang = positions.astype(F32)[..., None] * inv_freq
    cos, sin = jnp.cos(ang), jnp.sin(ang)
    rest = ATTN_HEAD_DIM - ROPE_DIM
    ones = jnp.ones(ang.shape[:-1] + (rest,), F32)
    zeros = jnp.zeros(ang.shape[:-1] + (rest,), F32)
    zh = jnp.zeros_like(sin)
    reps = LANES // ATTN_HEAD_DIM
    tile = lambda parts: jnp.tile(jnp.concatenate(parts, axis=-1), (1, 1, reps))
    return tile([cos, cos, ones]), tile([-sin, zh, zeros]), tile([zh, sin, zeros])


def _arrange_w_in(w):
    ab0 = IN_MAIN - GDN_WIDTH
    ab1 = ab0 + 2 * GDN_HEADS
    pad = jnp.zeros(w.shape[:2] + (LANES - 2 * GDN_HEADS,), w.dtype)
    return jnp.concatenate([w[..., :ab0], w[..., ab1:], w[..., ab0:ab1], pad], axis=-1).astype(BF16)


def _lane_padded(v):
    return jnp.pad(v.astype(F32), ((0, 0), (0, LANES - v.shape[1])))[:, None, :]


def kernel(x, mem, positions, norm_mix_pre, norm_mix_post, w_in, conv_short, conv_gdn, gdn_a_log, gdn_dt_bias, gdn_norm, w_out, norm_mem, norm_xattn_pre, norm_xattn_post, w_xq, w_xkv, w_xo, norm_ffn_pre, norm_ffn_post, w_gate_up, w_down):
    b, s, d = x.shape
    m = mem.shape[1]
    depth = w_in.shape[0]
    cos_t, sa_t, sb_t = _rotary_tables(positions)
    row = lambda g: g[:, None, :]
    w_in16, w_out16 = _arrange_w_in(w_in), w_out.astype(BF16)
    w_xq16, w_xkv16, w_xo16 = w_xq.astype(BF16), w_xkv.astype(BF16), w_xo.astype(BF16)
    w_gu16, w_down16 = w_gate_up.astype(BF16), w_down.astype(BF16)
    alog_v, dtb_v = _lane_padded(gdn_a_log), _lane_padded(gdn_dt_bias)
    mem2 = mem.reshape(b * m, d)
    h = x.reshape(b * s, d)
    for l in range(depth):
        proj = norm_matmul(h, row(norm_mix_pre), w_in16, l, out_dtype=F32).reshape(b, s, IN_PAD)
        ya = dilated_attention(proj, cos_t, sa_t, sb_t)
        yc = short_conv(proj, conv_short, l)
        yg = gated_deltanet(proj, conv_gdn, alog_v, dtb_v, row(gdn_norm), l)
        h = out_proj(ya.reshape(b * s, -1), yc.reshape(b * s, -1), yg.reshape(b * s, -1),
                     w_out16, row(norm_mix_post), h, l)
        kv = norm_matmul(mem2, row(norm_mem), w_xkv16, l, out_dtype=BF16).reshape(b, m, 2 * d)
        h = cross_attention(h.reshape(b, s, d), kv, w_xq16, w_xo16, row(norm_xattn_pre), row(norm_xattn_post),
                            l).reshape(b * s, d)
        h = swiglu_ffn(h, w_gu16, w_down16, row(norm_ffn_pre), row(norm_ffn_post), l)
    return h.reshape(b, s, d)
```

```python
import functools

import jax
import jax.numpy as jnp
from jax import lax
from jax.experimental import pallas as pl
from jax.experimental.pallas import tpu as pltpu

F32 = jnp.float32
BF16 = jnp.bfloat16
EPS = 1e-6

ATTN_HEADS = 4
ATTN_HEAD_DIM = 64
ATTN_WIDTH = ATTN_HEADS * ATTN_HEAD_DIM
DILATED_PATTERNS = ((128, 1), (512, 4), (2048, 16))
QUERY_BLOCK = 128
ROPE_THETA = 500000.0
ROPE_DIM = ATTN_HEAD_DIM // 4
CONV_WIDTH = 256
CONV_K = 3
GDN_HEADS = 4
GDN_HEAD_DIM = 128
GDN_WIDTH = GDN_HEADS * GDN_HEAD_DIM
GDN_CONV_K = 4
GDN_CHUNK = 64
GDN_SUB = 16
XATTN_HEADS = 4
LANES = 128
ATTN_PLANES = ATTN_WIDTH // LANES
HEADS_PER_PLANE = LANES // ATTN_HEAD_DIM
HALO = 8
IN_MAIN = 3 * ATTN_WIDTH + 3 * CONV_WIDTH + 4 * GDN_WIDTH
VMEM_LIMIT = 56 * 1024 * 1024


def _params(n_grid):
    return pltpu.CompilerParams(dimension_semantics=("arbitrary",) * n_grid,
                                vmem_limit_bytes=VMEM_LIMIT)


def _rms(x, gain):
    return x * lax.rsqrt(jnp.mean(x * x, axis=-1, keepdims=True) + EPS) * gain


def _dot(a, b):
    return jnp.dot(a, b, preferred_element_type=F32)


def _dot_nt(a, b):
    return lax.dot_general(a, b, (((1,), (1,)), ((), ())), preferred_element_type=F32)


def _silu(x):
    return x * (1.0 / (1.0 + jnp.exp(-x)))


def _layer_block(stacked, layer):
    tail = stacked.shape[1:]
    return pl.BlockSpec((None,) + tail, lambda *_: (layer,) + (0,) * len(tail), pipeline_mode=pl.Buffered(1))


def _norm_matmul_kernel(x_ref, g_ref, w_ref, o_ref, *, col_chunk):
    xn = _rms(x_ref[...], g_ref[...]).astype(BF16)
    n = w_ref.shape[1]
    for c in range(0, n, col_chunk):
        e = min(c + col_chunk, n)
        o_ref[:, c:e] = _dot(xn, w_ref[:, c:e]).astype(o_ref.dtype)


def norm_matmul(x, gain, w, layer, *, out_dtype, row_tile=512, col_chunk=512):
    t, d = x.shape
    n = w.shape[2]
    return pl.pallas_call(
        functools.partial(_norm_matmul_kernel, col_chunk=col_chunk),
        grid=(t // row_tile,),
        in_specs=[pl.BlockSpec((row_tile, d), lambda i: (i, 0)), _layer_block(gain, layer), _layer_block(w, layer)],
        out_specs=pl.BlockSpec((row_tile, n), lambda i: (i, 0)),
        out_shape=jax.ShapeDtypeStruct((t, n), out_dtype),
        compiler_params=_params(1),
        name="norm_matmul",
    )(x, gain, w)


def _in_proj_kernel(x_ref, g_ref, w_ref, cs_ref, cg_ref, attn_ref, yc_ref, gqkv_ref, gate_ref, ab_ref,
                    win_c_ref, halo_c_ref, win_g_ref, halo_g_ref, *, tiles_per_seq, col_chunk):
    i = pl.program_id(0)
    tm = x_ref.shape[0]
    cw = CONV_WIDTH

    @pl.when(i == 0)
    def _():
        halo_c_ref[...] = jnp.zeros_like(halo_c_ref)
        halo_g_ref[...] = jnp.zeros_like(halo_g_ref)

    first = (i % tiles_per_seq) == 0
    xn = _rms(x_ref[...], g_ref[...]).astype(BF16)

    def store_to(ref, lo):
        def epilogue(p, c0, c1):
            ref[:, c0 - lo:c1 - lo] = p
        return epilogue

    def short_conv(pb, pc, px, c0, c1):
        z = pc * px
        win_c_ref[0:HALO, :] = jnp.where(first, 0.0, halo_c_ref[...])
        win_c_ref[HALO:HALO + tm, :] = z
        halo_c_ref[...] = z[tm - HALO:tm, :]
        y = sum(cs_ref[j:j + 1, :] * win_c_ref[pl.ds(HALO - (CONV_K - 1) + j, tm), :] for j in range(CONV_K))
        yc_ref[...] = (pb * y).astype(yc_ref.dtype)

    gdn0 = 3 * ATTN_WIDTH + 3 * cw

    def gdn_conv(p, c0, c1):
        lo, hi = c0 - gdn0, c1 - gdn0
        win_g_ref[0:HALO, lo:hi] = jnp.where(first, 0.0, halo_g_ref[:, lo:hi])
        win_g_ref[HALO:HALO + tm, lo:hi] = p
        halo_g_ref[:, lo:hi] = p[tm - HALO:tm, :]
        y = sum(cg_ref[j:j + 1, lo:hi] * win_g_ref[pl.ds(HALO - (GDN_CONV_K - 1) + j, tm), lo:hi]
                for j in range(GDN_CONV_K))
        gqkv_ref[:, lo:hi] = _silu(y)

    def chunks(lo, hi, width=col_chunk):
        return [(c0, min(c0 + width, hi)) for c0 in range(lo, hi, width)]

    conv0 = 3 * ATTN_WIDTH
    gate0 = gdn0 + 3 * GDN_WIDTH
    ab0 = gate0 + GDN_WIDTH
    tasks = ([([rng], store_to(attn_ref, 0)) for rng in chunks(0, conv0)]
             + [(chunks(conv0, gdn0, cw), short_conv)]
             + [([rng], gdn_conv) for rng in chunks(gdn0, gate0)]
             + [([(gate0, ab0)], store_to(gate_ref, gate0)), ([(ab0, ab0 + LANES)], store_to(ab_ref, ab0))])
    for ranges, epilogue in tasks:
        epilogue(*[_dot(xn, w_ref[:, c0:c1]) for c0, c1 in ranges], ranges[0][0], ranges[-1][1])


def in_proj(x, gain, w, conv_short, conv_gdn, layer, seq, row_tile=512, col_chunk=512):
    t, d = x.shape
    assert seq % row_tile == 0 and (3 * GDN_WIDTH) % col_chunk == 0
    rows = lambda width: pl.BlockSpec((row_tile, width), lambda i: (i, 0))
    widths = (3 * ATTN_WIDTH, CONV_WIDTH, 3 * GDN_WIDTH, GDN_WIDTH, LANES)
    dtypes = (F32, BF16, F32, F32, F32)
    return pl.pallas_call(
        functools.partial(_in_proj_kernel, tiles_per_seq=seq // row_tile, col_chunk=col_chunk),
        grid=(t // row_tile,),
        in_specs=[rows(d), _layer_block(gain, layer), _layer_block(w, layer), _layer_block(conv_short, layer),
                  _layer_block(conv_gdn, layer)],
        out_specs=[rows(wd) for wd in widths],
        out_shape=[jax.ShapeDtypeStruct((t, wd), dt) for wd, dt in zip(widths, dtypes)],
        scratch_shapes=[pltpu.VMEM((HALO + row_tile, CONV_WIDTH), F32), pltpu.VMEM((HALO, CONV_WIDTH), F32),
                        pltpu.VMEM((HALO + row_tile, 3 * GDN_WIDTH), F32), pltpu.VMEM((HALO, 3 * GDN_WIDTH), F32)],
        compiler_params=_params(1),
        name="in_proj",
    )(x, gain, w, conv_short, conv_gdn)


def _attn_kernel(qkv_ref, cos_ref, sa_ref, sb_ref, o_ref, qr_ref, kr_ref, vr_ref, op_ref, lse_ref, *,
                 row_chunk, blocks_per_iter):
    assert HEADS_PER_PLANE == 2
    seq = qkv_ref.shape[1]
    qb = QUERY_BLOCK

    def rotate(i, carry):
        rows = pl.ds(pl.multiple_of(i * row_chunk, row_chunk), row_chunk)
        c, sa, sb = cos_ref[0, rows, :], sa_ref[0, rows, :], sb_ref[0, rows, :]
        for pln in range(ATTN_PLANES):
            for src, dst, scale in ((0, qr_ref, ATTN_HEAD_DIM ** -0.5), (ATTN_WIDTH, kr_ref, 1.0)):
                x = qkv_ref[0, rows, src + pln * LANES:src + (pln + 1) * LANES]
                rot = (x * c + pltpu.roll(x, LANES - ROPE_DIM // 2, 1) * sa
                       + pltpu.roll(x, ROPE_DIM // 2, 1) * sb)
                dst[pln, rows, :] = rot * scale
            vr_ref[pln, rows, :] = qkv_ref[0, rows, 2 * ATTN_WIDTH + pln * LANES:2 * ATTN_WIDTH + (pln + 1) * LANES]
        return carry

    lax.fori_loop(0, seq // row_chunk, rotate, 0)

    lane = lax.broadcasted_iota(jnp.int32, (1, LANES), 1)
    head_masks = [(lane >= ATTN_HEAD_DIM * h) & (lane < ATTN_HEAD_DIM * (h + 1)) for h in range(HEADS_PER_PLANE)]
    qi = lax.broadcasted_iota(jnp.int32, (qb, 2 * qb), 0)
    kj = lax.broadcasted_iota(jnp.int32, (qb, 2 * qb), 1)
    dist = qi + qb - kj

    for p, (window, dil) in enumerate(DILATED_PATTERNS):
        n_back = window // dil
        nb = seq // dil // qb
        band = (dist >= 0) & (dist <= n_back)

        def blocks(it, carry, p=p, dil=dil, nb=nb, band=band):
            dests, masks, qs, ks, vs = [], [], [], [], []
            for u in range(blocks_per_iter):
                blk = it * blocks_per_iter + u
                r = blk // nb
                n = blk % nb
                cur = r + n * (qb * dil)
                prv = r + jnp.maximum(n - 1, 0) * (qb * dil)
                if dil == 1:
                    rows_c, rows_p = pl.ds(pl.multiple_of(cur, qb), qb), pl.ds(pl.multiple_of(prv, qb), qb)
                else:
                    rows_c, rows_p = pl.ds(cur, qb, stride=dil), pl.ds(prv, qb, stride=dil)
                mask = band & ((kj >= qb) | (n > 0))
                for pln in range(ATTN_PLANES):
                    dests.append((pln, rows_c))
                    masks.append(mask)
                    qs.append(qr_ref[pln, rows_c, :])
                    ks.append(jnp.concatenate([kr_ref[pln, rows_p, :], kr_ref[pln, rows_c, :]], axis=0).astype(BF16))
                    vs.append(jnp.concatenate([vr_ref[pln, rows_p, :], vr_ref[pln, rows_c, :]], axis=0).astype(BF16))
            heads = range(HEADS_PER_PLANE)
            qh = [[jnp.where(head_masks[h], q, 0.0).astype(BF16) for h in heads] for q in qs]
            s = [[jnp.where(mask, _dot_nt(x, k), -jnp.inf) for x in xs] for xs, k, mask in zip(qh, ks, masks)]
            m = [[jnp.max(x, axis=-1, keepdims=True) for x in xs] for xs in s]
            pexp = [[jnp.exp(x - y) for x, y in zip(xs, ys)] for xs, ys in zip(s, m)]
            l = [[jnp.sum(x, axis=-1, keepdims=True) for x in xs] for xs in pexp]
            oh = [[_dot(x.astype(BF16), v) for x in xs] for xs, v in zip(pexp, vs)]
            for (pln, rows_c), os, ls, ms in zip(dests, oh, l, m):
                op_ref[p, pln, rows_c, :] = jnp.where(head_masks[0], os[0] / ls[0], os[1] / ls[1])
                lse_ref[p, pln, rows_c, :] = jnp.where(head_masks[0], ms[0] + jnp.log(ls[0]), ms[1] + jnp.log(ls[1]))
            return carry

        lax.fori_loop(0, dil * nb // blocks_per_iter, blocks, 0)

    def combine(i, carry):
        rows = pl.ds(pl.multiple_of(i * row_chunk, row_chunk), row_chunk)
        for pln in range(ATTN_PLANES):
            lses = [lse_ref[p, pln, rows, :] for p in range(len(DILATED_PATTERNS))]
            m = functools.reduce(jnp.maximum, lses)
            es = [jnp.exp(x - m) for x in lses]
            num = sum(e * op_ref[p, pln, rows, :] for p, e in enumerate(es))
            o_ref[0, rows, pln * LANES:(pln + 1) * LANES] = (num / sum(es)).astype(o_ref.dtype)
        return carry

    lax.fori_loop(0, seq // row_chunk, combine, 0)


def dilated_attention(qkv, cos_t, sa_t, sb_t):
    b, s, _ = qkv.shape
    assert all(s % (QUERY_BLOCK * d) == 0 and w // d <= QUERY_BLOCK for w, d in DILATED_PATTERNS)
    n_pat = len(DILATED_PATTERNS)
    tab = pl.BlockSpec((1, s, LANES), lambda i: (i, 0, 0))
    plane = pltpu.VMEM((ATTN_PLANES, s, LANES), F32)
    stats = pltpu.VMEM((n_pat, ATTN_PLANES, s, LANES), F32)
    return pl.pallas_call(
        functools.partial(_attn_kernel, row_chunk=256, blocks_per_iter=2),
        grid=(b,),
        in_specs=[pl.BlockSpec((1, s, 3 * ATTN_WIDTH), lambda i: (i, 0, 0)), tab, tab, tab],
        out_specs=pl.BlockSpec((1, s, ATTN_WIDTH), lambda i: (i, 0, 0)),
        out_shape=jax.ShapeDtypeStruct((b, s, ATTN_WIDTH), BF16),
        scratch_shapes=[plane, plane, plane, stats, stats],
        compiler_params=_params(1),
        name="dilated_attention",
    )(qkv, cos_t, sa_t, sb_t)


def _mm(a16, b16):
    return jnp.dot(a16, b16, preferred_element_type=F32)


def _lane_sum(x, ones16):
    hi = x.astype(BF16)
    lo = (x - hi.astype(F32)).astype(BF16)
    return _mm(hi, ones16) + _mm(lo, ones16)


def _unit_lower_inverse_minus_eye(a_list, diag_blocks):
    c = a_list[0].shape[0]
    assert c // GDN_SUB == 4
    ds = [jnp.where(diag_blocks, a, 0.0) for a in a_list]
    es = [a - d for a, d in zip(a_list, ds)]
    ns = [-d for d in ds]
    d16 = [d.astype(BF16) for d in ds]
    ps = [_mm(d, d) for d in d16]
    n_factors = GDN_SUB.bit_length() - 2
    for f in range(n_factors):
        p16 = [p.astype(BF16) for p in ps]
        if f + 1 < n_factors:
            both = [_mm(jnp.concatenate([n.astype(BF16), p], axis=0), p) for n, p in zip(ns, p16)]
            ns = [n + p + b[:c] for n, p, b in zip(ns, ps, both)]
            ps = [b[c:] for b in both]
        else:
            ns = [n + p + _mm(n.astype(BF16), q) for n, p, q in zip(ns, ps, p16)]
    e16 = [e.astype(BF16) for e in es]
    ms = [e + _mm(n.astype(BF16), q) for e, n, q in zip(es, ns, e16)]
    m16 = [m.astype(BF16) for m in ms]
    m2s = [_mm(m, m) for m in m16]
    qs = [m2 - m - _mm(q, m2.astype(BF16)) for m, m2, q in zip(ms, m2s, m16)]
    return [q + n + _mm(q.astype(BF16), n.astype(BF16)) for q, n in zip(qs, ns)]


def _gdn_kernel(qkv_ref, gate_ref, ab_ref, alog_ref, dtb_ref, nw_ref, o_ref, state_ref, *, group):
    seq = qkv_ref.shape[1]
    c, dh, nh = GDN_CHUNK, GDN_HEAD_DIM, GDN_HEADS
    rows_g = group * c
    row = lax.broadcasted_iota(jnp.int32, (c, c), 0)
    col = lax.broadcasted_iota(jnp.int32, (c, c), 1)
    causal = row >= col
    strict = row > col
    diag_blocks = (row // GDN_SUB) == (col // GDN_SUB)
    grow = lax.broadcasted_iota(jnp.int32, (rows_g, rows_g), 0)
    gcol = lax.broadcasted_iota(jnp.int32, (rows_g, rows_g), 1)
    tril_group = ((grow >= gcol) & (grow // c == gcol // c)).astype(BF16)
    ones16 = jnp.ones((dh, dh), BF16)
    state_ref[...] = jnp.zeros_like(state_ref)
    units = [(j, h) for j in range(group) for h in range(nh)]

    def body(gi, carry):
        base = pl.multiple_of(gi * rows_g, rows_g)
        rows = pl.ds(base, rows_g)
        xc = qkv_ref[0, rows, :]

        def l2_normalised(tile, scale):
            x = xc[:, tile * dh:(tile + 1) * dh]
            return x * (lax.rsqrt(_lane_sum(x * x, ones16) + EPS) * scale)

        qn = [l2_normalised(h, dh ** -0.5) for h in range(nh)]
        kn = [l2_normalised(nh + h, 1.0) for h in range(nh)]

        ab = ab_ref[0, rows, :]
        z = ab + dtb_ref[...]
        softplus = jnp.maximum(z, 0.0) + jnp.log1p(jnp.exp(-jnp.abs(z)))
        g_all = -jnp.exp(alog_ref[...]) * softplus
        beta_all = 1.0 / (1.0 + jnp.exp(-ab))
        g_hi = g_all.astype(BF16)
        g_r = g_all - g_hi.astype(F32)
        g_mid = g_r.astype(BF16)
        g_lo = (g_r - g_mid.astype(F32)).astype(BF16)
        decay_all = _mm(tril_group, g_hi) + _mm(tril_group, g_mid) + _mm(tril_group, g_lo)
        decay_t = decay_all.T
        beta_t = beta_all.T

        def lane_broadcast(row_vec):
            return jnp.broadcast_to(row_vec, (dh, c)).T

        qs, ks, vbs, kbs, rels, dcols, edecs, dlasts = [], [], [], [], [], [], [], []
        for j, h in units:
            r0 = j * c
            q = qn[h][r0:r0 + c]
            k = kn[h][r0:r0 + c]
            v = xc[r0:r0 + c, 2 * GDN_WIDTH + h * dh:2 * GDN_WIDTH + (h + 1) * dh]
            drow = decay_t[h:h + 1, r0:r0 + c]
            dcol = lane_broadcast(drow)
            beta = lane_broadcast(beta_t[nh + h:nh + h + 1, r0:r0 + c])
            qs.append(q)
            ks.append(k)
            vbs.append(v * beta)
            kbs.append(k * beta)
            rels.append(jnp.exp(jnp.where(causal, dcol[:, :c] - drow, -jnp.inf)))
            dcols.append(dcol)
            edecs.append(jnp.exp(dcol))
            dlasts.append(dcol[c - 1:c, :])

        k16 = [k.astype(BF16) for k in ks]
        kq = [_dot_nt(jnp.concatenate([kb, q], axis=0).astype(BF16), k) for kb, q, k in zip(kbs, qs, k16)]
        a_list = [jnp.where(strict, x[:c] * rel, 0.0) for x, rel in zip(kq, rels)]
        attn16 = [jnp.where(causal, x[c:] * rel, 0.0).astype(BF16) for x, rel in zip(kq, rels)]
        n_list = _unit_lower_inverse_minus_eye(a_list, diag_blocks)
        rhs = [jnp.concatenate([vb, kb * ed], axis=1) for vb, kb, ed in zip(vbs, kbs, edecs)]
        sol = [r + _mm(n.astype(BF16), r.astype(BF16)) for r, n in zip(rhs, n_list)]
        wq16 = [jnp.concatenate([x[:, dh:], q * ed], axis=0).astype(BF16) for x, q, ed in zip(sol, qs, edecs)]
        ak16 = [jnp.concatenate([at, (k * jnp.exp(dl - dc)).T.astype(BF16)], axis=0)
                for at, k, dl, dc in zip(attn16, ks, dlasts, dcols)]

        states = [state_ref[h] for h in range(nh)]
        for j in range(group):
            idx = [j * nh + h for h in range(nh)]
            ws_qs = [_mm(wq16[i], states[h].astype(BF16)) for h, i in enumerate(idx)]
            v16 = [(sol[i][:, :dh] - x[:c]).astype(BF16) for i, x in zip(idx, ws_qs)]
            av_kv = [_mm(ak16[i], v) for i, v in zip(idx, v16)]
            states = [s * jnp.exp(dlasts[i]) + y[c:] for s, i, y in zip(states, idx, av_kv)]
            out_rows = pl.ds(base + j * c, c)
            for h in range(nh):
                o = ws_qs[h][c:] + av_kv[h][:c]
                gate = gate_ref[0, out_rows, h * dh:(h + 1) * dh]
                o_ref[0, out_rows, h * dh:(h + 1) * dh] = (_rms(o, nw_ref[...]) * _silu(gate)).astype(o_ref.dtype)
        for h in range(nh):
            state_ref[h] = states[h]
        return carry

    lax.fori_loop(0, seq // rows_g, body, 0)


def gated_deltanet(qkv, gate, ab, alog_v, dtb_v, norm_w, layer, group=4):
    b, s, _ = qkv.shape
    seq_block = lambda width: pl.BlockSpec((1, s, width), lambda i: (i, 0, 0))
    return pl.pallas_call(
        functools.partial(_gdn_kernel, group=group),
        grid=(b,),
        in_specs=[seq_block(3 * GDN_WIDTH), seq_block(GDN_WIDTH), seq_block(LANES),
                  _layer_block(alog_v, layer), _layer_block(dtb_v, layer), _layer_block(norm_w, layer)],
        out_specs=seq_block(GDN_WIDTH),
        out_shape=jax.ShapeDtypeStruct((b, s, GDN_WIDTH), BF16),
        scratch_shapes=[pltpu.VMEM((GDN_HEADS, GDN_HEAD_DIM, GDN_HEAD_DIM), F32)],
        compiler_params=_params(1),
        name="gated_deltanet",
    )(qkv, gate, ab, alog_v, dtb_v, norm_w)


def _post_mixer_kernel(ya_ref, yc_ref, yg_ref, h_ref, kv_ref, wout_ref, wq_ref, wo_ref, wgu_ref, wd_ref,
                       g_mix_ref, g_xpre_ref, g_xpost_ref, g_fpre_ref, g_fpost_ref, o_ref, acc_ref, *, hid_chunk):
    h = h_ref[0]
    d = h.shape[-1]
    a_w, c_w = ya_ref.shape[2], yc_ref.shape[2]
    mix = (_dot(ya_ref[0], wout_ref[0:a_w, :]) + _dot(yc_ref[0], wout_ref[a_w:a_w + c_w, :])
           + _dot(yg_ref[0], wout_ref[a_w + c_w:, :]))
    h = h + _rms(mix, g_mix_ref[...])
    dh = d // XATTN_HEADS
    q = _dot(_rms(h, g_xpre_ref[...]).astype(BF16), wq_ref[...]).astype(BF16)
    heads = []
    for i in range(XATTN_HEADS):
        k = kv_ref[0, :, i * dh:(i + 1) * dh]
        v = kv_ref[0, :, d + i * dh:d + (i + 1) * dh]
        s = _dot_nt(q[:, i * dh:(i + 1) * dh], k) * (dh ** -0.5)
        p = jnp.exp(s - jnp.max(s, axis=-1, keepdims=True))
        l = jnp.sum(p, axis=-1, keepdims=True)
        heads.append((_dot(p.astype(BF16), v) / l).astype(BF16))
    h = h + _rms(_dot(jnp.concatenate(heads, axis=-1), wo_ref[...]), g_xpost_ref[...])
    hidden = wd_ref.shape[0]
    hn = _rms(h, g_fpre_ref[...]).astype(BF16)
    for idx, c in enumerate(range(0, hidden, hid_chunk)):
        e = min(c + hid_chunk, hidden)
        gate = _dot(hn, wgu_ref[:, c:e])
        up = _dot(hn, wgu_ref[:, hidden + c:hidden + e])
        part = _dot((_silu(gate) * up).astype(BF16), wd_ref[c:e, :])
        if idx == 0:
            acc_ref[...] = part
        else:
            acc_ref[...] += part
    o_ref[0] = h + _rms(acc_ref[...], g_fpost_ref[...])


def post_mixer(ya, yc, yg, h, kv, weights, gains, layer, row_tile=512, hid_chunk=512):
    b, s, d = h.shape
    m = kv.shape[1]
    tile = lambda width: pl.BlockSpec((1, row_tile, width), lambda i, j: (i, j, 0))
    return pl.pallas_call(
        functools.partial(_post_mixer_kernel, hid_chunk=hid_chunk),
        grid=(b, s // row_tile),
        in_specs=[tile(ya.shape[2]), tile(yc.shape[2]), tile(yg.shape[2]), tile(d),
                  pl.BlockSpec((1, m, 2 * d), lambda i, j: (i, 0, 0))]
                 + [_layer_block(w, layer) for w in weights] + [_layer_block(g, layer) for g in gains],
        out_specs=tile(d),
        out_shape=jax.ShapeDtypeStruct((b, s, d), F32),
        scratch_shapes=[pltpu.VMEM((row_tile, d), F32)],
        compiler_params=_params(2),
        name="post_mixer",
    )(ya, yc, yg, h, kv, *weights, *gains)


def _rotary_tables(positions):
    inv_freq = jnp.float32(ROPE_THETA) ** (-jnp.arange(0, ROPE_DIM, 2, dtype=F32) / ROPE_DIM)
    ang = positions.astype(F32)[..., None] * inv_freq
    cos, sin = jnp.cos(ang), jnp.sin(ang)
    rest = ATTN_HEAD_DIM - ROPE_DIM
    ones = jnp.ones(ang.shape[:-1] + (rest,), F32)
    zeros = jnp.zeros(ang.shape[:-1] + (rest,), F32)
    zh = jnp.zeros_like(sin)
    reps = LANES // ATTN_HEAD_DIM
    tile = lambda parts: jnp.tile(jnp.concatenate(parts, axis=-1), (1, 1, reps))
    return tile([cos, cos, ones]), tile([-sin, zh, zeros]), tile([zh, sin, zeros])


def _arrange_w_in(w):
    ab0 = IN_MAIN - GDN_WIDTH
    ab1 = ab0 + 2 * GDN_HEADS
    w = w.astype(BF16)
    pad = jnp.zeros(w.shape[:2] + (LANES - 2 * GDN_HEADS,), BF16)
    return jnp.concatenate([w[..., :ab0], w[..., ab1:], w[..., ab0:ab1], pad], axis=-1)


def _lane_padded(v):
    return jnp.pad(v.astype(F32), ((0, 0), (0, LANES - v.shape[1])))[:, None, :]


def kernel(x, mem, positions, norm_mix_pre, norm_mix_post, w_in, conv_short, conv_gdn, gdn_a_log, gdn_dt_bias, gdn_norm, w_out, norm_mem, norm_xattn_pre, norm_xattn_post, w_xq, w_xkv, w_xo, norm_ffn_pre, norm_ffn_post, w_gate_up, w_down):
    b, s, d = x.shape
    m = mem.shape[1]
    depth = w_in.shape[0]
    cos_t, sa_t, sb_t = _rotary_tables(positions)
    row = lambda g: g[:, None, :]
    w_in16, w_xkv16 = _arrange_w_in(w_in), w_xkv.astype(BF16)
    alog_v, dtb_v = _lane_padded(gdn_a_log), _lane_padded(gdn_dt_bias)
    mem2 = mem.reshape(b * m, d)
    weights = (w_out.astype(BF16), w_xq.astype(BF16), w_xo.astype(BF16), w_gate_up.astype(BF16), w_down.astype(BF16))
    gains = tuple(row(g) for g in (norm_mix_post, norm_xattn_pre, norm_xattn_post, norm_ffn_pre, norm_ffn_post))
    h = x
    for l in range(depth):
        qkv_a, yc, qkv_g, gate, ab = in_proj(h.reshape(b * s, d), row(norm_mix_pre), w_in16, conv_short, conv_gdn, l, s)
        ya = dilated_attention(qkv_a.reshape(b, s, -1), cos_t, sa_t, sb_t)
        yg = gated_deltanet(qkv_g.reshape(b, s, -1), gate.reshape(b, s, -1), ab.reshape(b, s, -1),
                            alog_v, dtb_v, row(gdn_norm), l)
        kv = norm_matmul(mem2, row(norm_mem), w_xkv16, l, out_dtype=BF16).reshape(b, m, 2 * d)
        h = post_mixer(ya, yc.reshape(b, s, -1), yg, h, kv, weights, gains, l)
    return h
```

```python
import functools

import jax
import jax.numpy as jnp
from jax import lax
from jax.experimental import pallas as pl
from jax.experimental.pallas import tpu as pltpu

F32 = jnp.float32
BF16 = jnp.bfloat16
EPS = 1e-6

ATTN_HEADS = 4
ATTN_HEAD_DIM = 64
ATTN_WIDTH = ATTN_HEADS * ATTN_HEAD_DIM
DILATED_PATTERNS = ((128, 1), (512, 4), (2048, 16))
QUERY_BLOCK = 128
ROPE_THETA = 500000.0
ROPE_DIM = ATTN_HEAD_DIM // 4
CONV_WIDTH = 256
CONV_K = 3
GDN_HEADS = 4
GDN_HEAD_DIM = 128
GDN_WIDTH = GDN_HEADS * GDN_HEAD_DIM
GDN_CONV_K = 4
GDN_CHUNK = 64
GDN_SUB = 16
XATTN_HEADS = 4
LANES = 128
ATTN_PLANES = ATTN_WIDTH // LANES
HEADS_PER_PLANE = LANES // ATTN_HEAD_DIM
HALO = 8
IN_MAIN = 3 * ATTN_WIDTH + 3 * CONV_WIDTH + 4 * GDN_WIDTH
VMEM_LIMIT = 56 * 1024 * 1024


def _params(n_grid):
    return pltpu.CompilerParams(dimension_semantics=("arbitrary",) * n_grid,
                                vmem_limit_bytes=VMEM_LIMIT)


def _rms(x, gain):
    return x * lax.rsqrt(jnp.mean(x * x, axis=-1, keepdims=True) + EPS) * gain


def _dot(a, b):
    return jnp.dot(a, b, preferred_element_type=F32)


def _dot_nt(a, b):
    return lax.dot_general(a, b, (((1,), (1,)), ((), ())), preferred_element_type=F32)


def _silu(x):
    return x * (1.0 / (1.0 + jnp.exp(-x)))


def _layer_block(stacked, layer):
    tail = stacked.shape[1:]
    return pl.BlockSpec((None,) + tail, lambda *_: (layer,) + (0,) * len(tail), pipeline_mode=pl.Buffered(1))


def _norm_matmul_kernel(x_ref, g_ref, w_ref, o_ref, *, col_chunk):
    xn = _rms(x_ref[...], g_ref[...]).astype(BF16)
    n = w_ref.shape[1]
    for c in range(0, n, col_chunk):
        e = min(c + col_chunk, n)
        o_ref[:, c:e] = _dot(xn, w_ref[:, c:e]).astype(o_ref.dtype)


def norm_matmul(x, gain, w, layer, *, out_dtype, row_tile=512, col_chunk=512):
    t, d = x.shape
    n = w.shape[2]
    return pl.pallas_call(
        functools.partial(_norm_matmul_kernel, col_chunk=col_chunk),
        grid=(t // row_tile,),
        in_specs=[pl.BlockSpec((row_tile, d), lambda i: (i, 0)), _layer_block(gain, layer), _layer_block(w, layer)],
        out_specs=pl.BlockSpec((row_tile, n), lambda i: (i, 0)),
        out_shape=jax.ShapeDtypeStruct((t, n), out_dtype),
        compiler_params=_params(1),
        name="norm_matmul",
    )(x, gain, w)


def _in_proj_kernel(x_ref, g_ref, w_ref, wgate_ref, wab_ref, cs_ref, cg_ref, cos_ref, sin_ref, attn_ref, yc_ref,
                    gqkv_ref, gate_ref, ab_ref, win_c_ref, halo_c_ref, win_g_ref, halo_g_ref, *,
                    tiles_per_seq, col_chunk):
    i = pl.program_id(0)
    tm = x_ref.shape[0]
    cw = CONV_WIDTH

    @pl.when(i == 0)
    def _():
        halo_c_ref[...] = jnp.zeros_like(halo_c_ref)
        halo_g_ref[...] = jnp.zeros_like(halo_g_ref)

    first = (i % tiles_per_seq) == 0
    xn = _rms(x_ref[...], g_ref[...]).astype(BF16)

    def store_to(ref, lo):
        def epilogue(p, c0, c1):
            ref[:, c0 - lo:c1 - lo] = p
        return epilogue

    lane = lax.broadcasted_iota(jnp.int32, (1, LANES), 1)
    low_half = (lane % ATTN_HEAD_DIM) < ROPE_DIM // 2

    def attn_planes(p, c0, c1):
        group = c0 // ATTN_WIDTH
        for pln in range(ATTN_PLANES):
            x = p[:, pln * LANES:(pln + 1) * LANES]
            if group < 2:
                partner = jnp.where(low_half, pltpu.roll(x, LANES - ROPE_DIM // 2, 1), pltpu.roll(x, ROPE_DIM // 2, 1))
                x = x * cos_ref[...] + partner * sin_ref[...]
            if group == 0:
                x = x * (ATTN_HEAD_DIM ** -0.5)
            attn_ref[group * ATTN_PLANES + pln] = x

    def short_conv(pb, pc, px, c0, c1):
        z = pc * px
        win_c_ref[0:HALO, :] = jnp.where(first, 0.0, halo_c_ref[...])
        win_c_ref[HALO:HALO + tm, :] = z
        halo_c_ref[...] = z[tm - HALO:tm, :]
        y = sum(cs_ref[j:j + 1, :] * win_c_ref[pl.ds(HALO - (CONV_K - 1) + j, tm), :] for j in range(CONV_K))
        yc_ref[...] = (pb * y).astype(yc_ref.dtype)

    gdn0 = 3 * ATTN_WIDTH + 3 * cw

    def gdn_conv(p, c0, c1):
        lo, hi = c0 - gdn0, c1 - gdn0
        win_g_ref[0:HALO, lo:hi] = jnp.where(first, 0.0, halo_g_ref[:, lo:hi])
        win_g_ref[HALO:HALO + tm, lo:hi] = p
        halo_g_ref[:, lo:hi] = p[tm - HALO:tm, :]
        y = sum(cg_ref[j:j + 1, lo:hi] * win_g_ref[pl.ds(HALO - (GDN_CONV_K - 1) + j, tm), lo:hi]
                for j in range(GDN_CONV_K))
        gqkv_ref[:, lo:hi] = _silu(y)

    def chunks(lo, hi, width=col_chunk):
        return [(c0, min(c0 + width, hi)) for c0 in range(lo, hi, width)]

    conv0 = 3 * ATTN_WIDTH
    gate0 = gdn0 + 3 * GDN_WIDTH
    tasks = ([(w_ref, [rng], attn_planes) for rng in chunks(0, conv0, ATTN_WIDTH)]
             + [(w_ref, chunks(conv0, gdn0, cw), short_conv)]
             + [(w_ref, [rng], gdn_conv) for rng in chunks(gdn0, gate0)]
             + [(wgate_ref, [(0, GDN_WIDTH)], store_to(gate_ref, 0)), (wab_ref, [(0, LANES)], store_to(ab_ref, 0))])
    for weights, ranges, epilogue in tasks:
        epilogue(*[_dot(xn, weights[:, c0:c1]) for c0, c1 in ranges], ranges[0][0], ranges[-1][1])


def in_proj(x, gain, w, w_gate, w_ab, conv_short, conv_gdn, cos_t, sin_t, layer, seq, row_tile=512, col_chunk=512):
    t, d = x.shape
    assert seq % row_tile == 0 and (3 * GDN_WIDTH) % col_chunk == 0
    rows = lambda width: pl.BlockSpec((row_tile, width), lambda i: (i, 0))
    widths = (CONV_WIDTH, 3 * GDN_WIDTH, GDN_WIDTH, LANES)
    dtypes = (BF16, F32, F32, F32)
    n_planes = 3 * ATTN_PLANES
    return pl.pallas_call(
        functools.partial(_in_proj_kernel, tiles_per_seq=seq // row_tile, col_chunk=col_chunk),
        grid=(t // row_tile,),
        in_specs=[rows(d), _layer_block(gain, layer), _layer_block(w, layer), _layer_block(w_gate, layer),
                  _layer_block(w_ab, layer), _layer_block(conv_short, layer), _layer_block(conv_gdn, layer),
                  rows(LANES), rows(LANES)],
        out_specs=[pl.BlockSpec((n_planes, row_tile, LANES), lambda i: (0, i, 0))] + [rows(wd) for wd in widths],
        out_shape=[jax.ShapeDtypeStruct((n_planes, t, LANES), F32)]
                  + [jax.ShapeDtypeStruct((t, wd), dt) for wd, dt in zip(widths, dtypes)],
        scratch_shapes=[pltpu.VMEM((HALO + row_tile, CONV_WIDTH), F32), pltpu.VMEM((HALO, CONV_WIDTH), F32),
                        pltpu.VMEM((HALO + row_tile, 3 * GDN_WIDTH), F32), pltpu.VMEM((HALO, 3 * GDN_WIDTH), F32)],
        compiler_params=_params(1),
        name="in_proj",
    )(x, gain, w, w_gate, w_ab, conv_short, conv_gdn, cos_t, sin_t)


def _attn_kernel(x_ref, o_ref, op_ref, lse_ref, *, row_chunk, blocks_per_iter):
    assert HEADS_PER_PLANE == 2
    seq = x_ref.shape[1]
    qb = QUERY_BLOCK
    k0, v0 = ATTN_PLANES, 2 * ATTN_PLANES
    lane = lax.broadcasted_iota(jnp.int32, (1, LANES), 1)
    head_masks = [(lane >= ATTN_HEAD_DIM * h) & (lane < ATTN_HEAD_DIM * (h + 1)) for h in range(HEADS_PER_PLANE)]
    qi = lax.broadcasted_iota(jnp.int32, (qb, 2 * qb), 0)
    kj = lax.broadcasted_iota(jnp.int32, (qb, 2 * qb), 1)
    dist = qi + qb - kj

    for p, (window, dil) in enumerate(DILATED_PATTERNS):
        n_back = window // dil
        nb = seq // dil // qb
        band = (dist >= 0) & (dist <= n_back)

        def blocks(it, carry, p=p, dil=dil, nb=nb, band=band):
            dests, masks, qs, ks, vs = [], [], [], [], []
            for u in range(blocks_per_iter):
                blk = it * blocks_per_iter + u
                r = blk // nb
                n = blk % nb
                cur = r + n * (qb * dil)
                prv = r + jnp.maximum(n - 1, 0) * (qb * dil)
                if dil == 1:
                    rows_c, rows_p = pl.ds(pl.multiple_of(cur, qb), qb), pl.ds(pl.multiple_of(prv, qb), qb)
                else:
                    rows_c, rows_p = pl.ds(cur, qb, stride=dil), pl.ds(prv, qb, stride=dil)
                mask = band & ((kj >= qb) | (n > 0))
                for pln in range(ATTN_PLANES):
                    dests.append((pln, rows_c))
                    masks.append(mask)
                    qs.append(x_ref[pln, rows_c, :])
                    ks.append(jnp.concatenate([x_ref[k0 + pln, rows_p, :], x_ref[k0 + pln, rows_c, :]],
                                              axis=0).astype(BF16))
                    vs.append(jnp.concatenate([x_ref[v0 + pln, rows_p, :], x_ref[v0 + pln, rows_c, :]],
                                              axis=0).astype(BF16))
            heads = range(HEADS_PER_PLANE)
            qh = [[jnp.where(head_masks[h], q, 0.0).astype(BF16) for h in heads] for q in qs]
            s = [[jnp.where(mask, _dot_nt(x, k), -jnp.inf) for x in xs] for xs, k, mask in zip(qh, ks, masks)]
            m = [[jnp.max(x, axis=-1, keepdims=True) for x in xs] for xs in s]
            pexp = [[jnp.exp(x - y) for x, y in zip(xs, ys)] for xs, ys in zip(s, m)]
            l = [[jnp.sum(x, axis=-1, keepdims=True) for x in xs] for xs in pexp]
            oh = [[_dot(x.astype(BF16), v) for x in xs] for xs, v in zip(pexp, vs)]
            for (pln, rows_c), os, ls, ms in zip(dests, oh, l, m):
                op_ref[p, pln, rows_c, :] = jnp.where(head_masks[0], os[0] / ls[0], os[1] / ls[1])
                lse_ref[p, pln, rows_c, :] = jnp.where(head_masks[0], ms[0] + jnp.log(ls[0]), ms[1] + jnp.log(ls[1]))
            return carry

        lax.fori_loop(0, dil * nb // blocks_per_iter, blocks, 0)

    def combine(i, carry):
        rows = pl.ds(pl.multiple_of(i * row_chunk, row_chunk), row_chunk)
        for pln in range(ATTN_PLANES):
            lses = [lse_ref[p, pln, rows, :] for p in range(len(DILATED_PATTERNS))]
            m = functools.reduce(jnp.maximum, lses)
            es = [jnp.exp(x - m) for x in lses]
            num = sum(e * op_ref[p, pln, rows, :] for p, e in enumerate(es))
            o_ref[0, rows, pln * LANES:(pln + 1) * LANES] = (num / sum(es)).astype(o_ref.dtype)
        return carry

    lax.fori_loop(0, seq // row_chunk, combine, 0)


def dilated_attention(planes, batch):
    n_planes, t, _ = planes.shape
    s = t // batch
    assert all(s % (QUERY_BLOCK * d) == 0 and w // d <= QUERY_BLOCK for w, d in DILATED_PATTERNS)
    stats = pltpu.VMEM((len(DILATED_PATTERNS), ATTN_PLANES, s, LANES), F32)
    return pl.pallas_call(
        functools.partial(_attn_kernel, row_chunk=256, blocks_per_iter=2),
        grid=(batch,),
        in_specs=[pl.BlockSpec((n_planes, s, LANES), lambda i: (0, i, 0))],
        out_specs=pl.BlockSpec((1, s, ATTN_WIDTH), lambda i: (i, 0, 0)),
        out_shape=jax.ShapeDtypeStruct((batch, s, ATTN_WIDTH), BF16),
        scratch_shapes=[stats, stats],
        compiler_params=_params(1),
        name="dilated_attention",
    )(planes)


def _mm(a16, b16):
    return jnp.dot(a16, b16, preferred_element_type=F32)


def _lane_sum(x, ones16):
    hi = x.astype(BF16)
    lo = (x - hi.astype(F32)).astype(BF16)
    return _mm(hi, ones16) + _mm(lo, ones16)


def _unit_lower_inverse_minus_eye(a_list, diag_blocks):
    c = a_list[0].shape[0]
    assert c // GDN_SUB == 4
    ds = [jnp.where(diag_blocks, a, 0.0) for a in a_list]
    es = [a - d for a, d in zip(a_list, ds)]
    ns = [-d for d in ds]
    d16 = [d.astype(BF16) for d in ds]
    ps = [_mm(d, d) for d in d16]
    n_factors = GDN_SUB.bit_length() - 2
    for f in range(n_factors):
        p16 = [p.astype(BF16) for p in ps]
        if f + 1 < n_factors:
            both = [_mm(jnp.concatenate([n.astype(BF16), p], axis=0), p) for n, p in zip(ns, p16)]
            ns = [n + p + b[:c] for n, p, b in zip(ns, ps, both)]
            ps = [b[c:] for b in both]
        else:
            ns = [n + p + _mm(n.astype(BF16), q) for n, p, q in zip(ns, ps, p16)]
    e16 = [e.astype(BF16) for e in es]
    ms = [e + _mm(n.astype(BF16), q) for e, n, q in zip(es, ns, e16)]
    m16 = [m.astype(BF16) for m in ms]
    m2s = [_mm(m, m) for m in m16]
    qs = [m2 - m - _mm(q, m2.astype(BF16)) for m, m2, q in zip(ms, m2s, m16)]
    return [q + n + _mm(q.astype(BF16), n.astype(BF16)) for q, n in zip(qs, ns)]


def _gdn_kernel(qkv_ref, gate_ref, ab_ref, alog_ref, dtb_ref, nw_ref, o_ref, state_ref, *, group):
    seq = qkv_ref.shape[1]
    c, dh, nh = GDN_CHUNK, GDN_HEAD_DIM, GDN_HEADS
    rows_g = group * c
    row = lax.broadcasted_iota(jnp.int32, (c, c), 0)
    col = lax.broadcasted_iota(jnp.int32, (c, c), 1)
    causal = row >= col
    strict = row > col
    diag_blocks = (row // GDN_SUB) == (col // GDN_SUB)
    grow = lax.broadcasted_iota(jnp.int32, (rows_g, rows_g), 0)
    gcol = lax.broadcasted_iota(jnp.int32, (rows_g, rows_g), 1)
    tril_group = ((grow >= gcol) & (grow // c == gcol // c)).astype(BF16)
    ones16 = jnp.ones((dh, dh), BF16)
    state_ref[...] = jnp.zeros_like(state_ref)
    units = [(j, h) for j in range(group) for h in range(nh)]

    def body(gi, carry):
        base = pl.multiple_of(gi * rows_g, rows_g)
        rows = pl.ds(base, rows_g)
        xc = qkv_ref[0, rows, :]

        def l2_normalised(tile, scale):
            x = xc[:, tile * dh:(tile + 1) * dh]
            return x * (lax.rsqrt(_lane_sum(x * x, ones16) + EPS) * scale)

        qn = [l2_normalised(h, dh ** -0.5) for h in range(nh)]
        kn = [l2_normalised(nh + h, 1.0) for h in range(nh)]

        ab = ab_ref[0, rows, :]
        z = ab + dtb_ref[...]
        softplus = jnp.maximum(z, 0.0) + jnp.log1p(jnp.exp(-jnp.abs(z)))
        g_all = -jnp.exp(alog_ref[...]) * softplus
        beta_all = 1.0 / (1.0 + jnp.exp(-ab))
        g_hi = g_all.astype(BF16)
        g_r = g_all - g_hi.astype(F32)
        g_mid = g_r.astype(BF16)
        g_lo = (g_r - g_mid.astype(F32)).astype(BF16)
        decay_all = _mm(tril_group, g_hi) + _mm(tril_group, g_mid) + _mm(tril_group, g_lo)
        decay_t = decay_all.T
        beta_t = beta_all.T

        def lane_broadcast(row_vec):
            return jnp.broadcast_to(row_vec, (dh, c)).T

        qs, ks, vbs, kbs, rels, dcols, edecs, dlasts = [], [], [], [], [], [], [], []
        for j, h in units:
            r0 = j * c
            q = qn[h][r0:r0 + c]
            k = kn[h][r0:r0 + c]
            v = xc[r0:r0 + c, 2 * GDN_WIDTH + h * dh:2 * GDN_WIDTH + (h + 1) * dh]
            drow = decay_t[h:h + 1, r0:r0 + c]
            dcol = lane_broadcast(drow)
            beta = lane_broadcast(beta_t[nh + h:nh + h + 1, r0:r0 + c])
            qs.append(q)
            ks.append(k)
            vbs.append(v * beta)
            kbs.append(k * beta)
            rels.append(jnp.exp(jnp.where(causal, dcol[:, :c] - drow, -jnp.inf)))
            dcols.append(dcol)
            edecs.append(jnp.exp(dcol))
            dlasts.append(dcol[c - 1:c, :])

        k16 = [k.astype(BF16) for k in ks]
        kq = [_dot_nt(jnp.concatenate([kb, q], axis=0).astype(BF16), k) for kb, q, k in zip(kbs, qs, k16)]
        a_list = [jnp.where(strict, x[:c] * rel, 0.0) for x, rel in zip(kq, rels)]
        attn16 = [jnp.where(causal, x[c:] * rel, 0.0).astype(BF16) for x, rel in zip(kq, rels)]
        n_list = _unit_lower_inverse_minus_eye(a_list, diag_blocks)
        rhs = [jnp.concatenate([vb, kb * ed], axis=1) for vb, kb, ed in zip(vbs, kbs, edecs)]
        sol = [r + _mm(n.astype(BF16), r.astype(BF16)) for r, n in zip(rhs, n_list)]
        wq16 = [jnp.concatenate([x[:, dh:], q * ed], axis=0).astype(BF16) for x, q, ed in zip(sol, qs, edecs)]
        ak16 = [jnp.concatenate([at, (k * jnp.exp(dl - dc)).T.astype(BF16)], axis=0)
                for at, k, dl, dc in zip(attn16, ks, dlasts, dcols)]

        states = [state_ref[h] for h in range(nh)]
        for j in range(group):
            idx = [j * nh + h for h in range(nh)]
            ws_qs = [_mm(wq16[i], states[h].astype(BF16)) for h, i in enumerate(idx)]
            v16 = [(sol[i][:, :dh] - x[:c]).astype(BF16) for i, x in zip(idx, ws_qs)]
            av_kv = [_mm(ak16[i], v) for i, v in zip(idx, v16)]
            states = [s * jnp.exp(dlasts[i]) + y[c:] for s, i, y in zip(states, idx, av_kv)]
            out_rows = pl.ds(base + j * c, c)
            for h in range(nh):
                o = ws_qs[h][c:] + av_kv[h][:c]
                gate = gate_ref[0, out_rows, h * dh:(h + 1) * dh]
                o_ref[0, out_rows, h * dh:(h + 1) * dh] = (_rms(o, nw_ref[...]) * _silu(gate)).astype(o_ref.dtype)
        for h in range(nh):
            state_ref[h] = states[h]
        return carry

    lax.fori_loop(0, seq // rows_g, body, 0)


def gated_deltanet(qkv, gate, ab, alog_v, dtb_v, norm_w, layer, group=4):
    b, s, _ = qkv.shape
    seq_block = lambda width: pl.BlockSpec((1, s, width), lambda i: (i, 0, 0))
    return pl.pallas_call(
        functools.partial(_gdn_kernel, group=group),
        grid=(b,),
        in_specs=[seq_block(3 * GDN_WIDTH), seq_block(GDN_WIDTH), seq_block(LANES),
                  _layer_block(alog_v, layer), _layer_block(dtb_v, layer), _layer_block(norm_w, layer)],
        out_specs=seq_block(GDN_WIDTH),
        out_shape=jax.ShapeDtypeStruct((b, s, GDN_WIDTH), BF16),
        scratch_shapes=[pltpu.VMEM((GDN_HEADS, GDN_HEAD_DIM, GDN_HEAD_DIM), F32)],
        compiler_params=_params(1),
        name="gated_deltanet",
    )(qkv, gate, ab, alog_v, dtb_v, norm_w)


def _post_mixer_kernel(ya_ref, yc_ref, yg_ref, h_ref, kv_ref, wout_ref, wq_ref, wo_ref, wgu_ref, wd_ref,
                       g_mix_ref, g_xpre_ref, g_xpost_ref, g_fpre_ref, g_fpost_ref, o_ref, acc_ref, *, hid_chunk):
    h = h_ref[0]
    d = h.shape[-1]
    a_w, c_w = ya_ref.shape[2], yc_ref.shape[2]
    mix = (_dot(ya_ref[0], wout_ref[0:a_w, :]) + _dot(yc_ref[0], wout_ref[a_w:a_w + c_w, :])
           + _dot(yg_ref[0], wout_ref[a_w + c_w:, :]))
    h = h + _rms(mix, g_mix_ref[...])
    dh = d // XATTN_HEADS
    q = _dot(_rms(h, g_xpre_ref[...]).astype(BF16), wq_ref[...]).astype(BF16)
    heads = []
    for i in range(XATTN_HEADS):
        k = kv_ref[0, :, i * dh:(i + 1) * dh]
        v = kv_ref[0, :, d + i * dh:d + (i + 1) * dh]
        s = _dot_nt(q[:, i * dh:(i + 1) * dh], k) * (dh ** -0.5)
        p = jnp.exp(s - jnp.max(s, axis=-1, keepdims=True))
        l = jnp.sum(p, axis=-1, keepdims=True)
        heads.append((_dot(p.astype(BF16), v) / l).astype(BF16))
    h = h + _rms(_dot(jnp.concatenate(heads, axis=-1), wo_ref[...]), g_xpost_ref[...])
    hidden = wd_ref.shape[0]
    hn = _rms(h, g_fpre_ref[...]).astype(BF16)
    for idx, c in enumerate(range(0, hidden, hid_chunk)):
        e = min(c + hid_chunk, hidden)
        gate = _dot(hn, wgu_ref[:, c:e])
        up = _dot(hn, wgu_ref[:, hidden + c:hidden + e])
        part = _dot((_silu(gate) * up).astype(BF16), wd_ref[c:e, :])
        if idx == 0:
            acc_ref[...] = part
        else:
            acc_ref[...] += part
    o_ref[0] = h + _rms(acc_ref[...], g_fpost_ref[...])


def post_mixer(ya, yc, yg, h, kv, weights, gains, layer, row_tile=512, hid_chunk=512):
    b, s, d = h.shape
    m = kv.shape[1]
    tile = lambda width: pl.BlockSpec((1, row_tile, width), lambda i, j: (i, j, 0))
    return pl.pallas_call(
        functools.partial(_post_mixer_kernel, hid_chunk=hid_chunk),
        grid=(b, s // row_tile),
        in_specs=[tile(ya.shape[2]), tile(yc.shape[2]), tile(yg.shape[2]), tile(d),
                  pl.BlockSpec((1, m, 2 * d), lambda i, j: (i, 0, 0))]
                 + [_layer_block(w, layer) for w in weights] + [_layer_block(g, layer) for g in gains],
        out_specs=tile(d),
        out_shape=jax.ShapeDtypeStruct((b, s, d), F32),
        scratch_shapes=[pltpu.VMEM((row_tile, d), F32)],
        compiler_params=_params(2),
        name="post_mixer",
    )(ya, yc, yg, h, kv, *weights, *gains)


def _rotary_tables(positions):
    inv_freq = jnp.float32(ROPE_THETA) ** (-jnp.arange(0, ROPE_DIM, 2, dtype=F32) / ROPE_DIM)
    ang = positions.astype(F32)[..., None] * inv_freq
    cos, sin = jnp.cos(ang), jnp.sin(ang)
    rest = ATTN_HEAD_DIM - ROPE_DIM
    ones = jnp.ones(ang.shape[:-1] + (rest,), F32)
    zeros = jnp.zeros(ang.shape[:-1] + (rest,), F32)
    reps = LANES // ATTN_HEAD_DIM
    tile = lambda parts: jnp.tile(jnp.concatenate(parts, axis=-1), (1, 1, reps))
    return tile([cos, cos, ones]), tile([-sin, sin, zeros])


def _split_w_in(w):
    ab0 = IN_MAIN - GDN_WIDTH
    ab1 = ab0 + 2 * GDN_HEADS
    w = w.astype(BF16)
    return w, w[..., ab1:], jnp.pad(w[..., ab0:ab1], ((0, 0), (0, 0), (0, LANES - 2 * GDN_HEADS)))


def _lane_padded(v):
    return jnp.pad(v.astype(F32), ((0, 0), (0, LANES - v.shape[1])))[:, None, :]


def kernel(x, mem, positions, norm_mix_pre, norm_mix_post, w_in, conv_short, conv_gdn, gdn_a_log, gdn_dt_bias, gdn_norm, w_out, norm_mem, norm_xattn_pre, norm_xattn_post, w_xq, w_xkv, w_xo, norm_ffn_pre, norm_ffn_post, w_gate_up, w_down):
    b, s, d = x.shape
    m = mem.shape[1]
    depth = w_in.shape[0]
    cos_t, sin_t = (t.reshape(b * s, LANES) for t in _rotary_tables(positions))
    row = lambda g: g[:, None, :]
    (w_in16, w_gate16, w_ab16), w_xkv16 = _split_w_in(w_in), w_xkv.astype(BF16)
    alog_v, dtb_v = _lane_padded(gdn_a_log), _lane_padded(gdn_dt_bias)
    mem2 = mem.reshape(b * m, d)
    weights = (w_out.astype(BF16), w_xq.astype(BF16), w_xo.astype(BF16), w_gate_up.astype(BF16), w_down.astype(BF16))
    gains = tuple(row(g) for g in (norm_mix_post, norm_xattn_pre, norm_xattn_post, norm_ffn_pre, norm_ffn_post))
    h = x
    for l in range(depth):
        planes, yc, qkv_g, gate, ab = in_proj(h.reshape(b * s, d), row(norm_mix_pre), w_in16, w_gate16, w_ab16,
                                              conv_short, conv_gdn, cos_t, sin_t, l, s)
        ya = dilated_attention(planes, b)
        yg = gated_deltanet(qkv_g.reshape(b, s, -1), gate.reshape(b, s, -1), ab.reshape(b, s, -1),
                            alog_v, dtb_v, row(gdn_norm), l)
        kv = norm_matmul(mem2, row(norm_mem), w_xkv16, l, out_dtype=BF16).reshape(b, m, 2 * d)
        h = post_mixer(ya, yc.reshape(b, s, -1), yg, h, kv, weights, gains, l)
    return h
```

```python
import functools

import jax
import jax.numpy as jnp
from jax import lax
from jax.experimental import pallas as pl
from jax.experimental.pallas import tpu as pltpu

F32 = jnp.float32
BF16 = jnp.bfloat16
EPS = 1e-6

ATTN_HEADS = 4
ATTN_HEAD_DIM = 64
ATTN_WIDTH = ATTN_HEADS * ATTN_HEAD_DIM
DILATED_PATTERNS = ((128, 1), (512, 4), (2048, 16))
QUERY_BLOCK = 128
ROPE_THETA = 500000.0
ROPE_DIM = ATTN_HEAD_DIM // 4
CONV_WIDTH = 256
CONV_K = 3
GDN_HEADS = 4
GDN_HEAD_DIM = 128
GDN_WIDTH = GDN_HEADS * GDN_HEAD_DIM
GDN_CONV_K = 4
GDN_CHUNK = 64
GDN_SUB = 16
XATTN_HEADS = 4
LANES = 128
ATTN_PLANES = ATTN_WIDTH // LANES
HEADS_PER_PLANE = LANES // ATTN_HEAD_DIM
HALO = 8
IN_MAIN = 3 * ATTN_WIDTH + 3 * CONV_WIDTH + 4 * GDN_WIDTH
VMEM_LIMIT = 56 * 1024 * 1024


def _params(n_grid):
    return pltpu.CompilerParams(dimension_semantics=("arbitrary",) * n_grid,
                                vmem_limit_bytes=VMEM_LIMIT)


def _rms(x, gain):
    return x * lax.rsqrt(jnp.mean(x * x, axis=-1, keepdims=True) + EPS) * gain


def _dot(a, b):
    return jnp.dot(a, b, preferred_element_type=F32)


def _dot_nt(a, b):
    return lax.dot_general(a, b, (((1,), (1,)), ((), ())), preferred_element_type=F32)


def _silu(x):
    return x * (1.0 / (1.0 + jnp.exp(-x)))


def _layer_block(stacked, layer):
    tail = stacked.shape[1:]
    return pl.BlockSpec((None,) + tail, lambda *_: (layer,) + (0,) * len(tail), pipeline_mode=pl.Buffered(1))


def _norm_matmul_kernel(x_ref, g_ref, w_ref, o_ref, *, col_chunk):
    xn = _rms(x_ref[...], g_ref[...]).astype(BF16)
    n = w_ref.shape[1]
    for c in range(0, n, col_chunk):
        e = min(c + col_chunk, n)
        o_ref[:, c:e] = _dot(xn, w_ref[:, c:e]).astype(o_ref.dtype)


def norm_matmul(x, gain, w, layer, *, out_dtype, row_tile=512, col_chunk=512):
    t, d = x.shape
    n = w.shape[2]
    return pl.pallas_call(
        functools.partial(_norm_matmul_kernel, col_chunk=col_chunk),
        grid=(t // row_tile,),
        in_specs=[pl.BlockSpec((row_tile, d), lambda i: (i, 0)), _layer_block(gain, layer), _layer_block(w, layer)],
        out_specs=pl.BlockSpec((row_tile, n), lambda i: (i, 0)),
        out_shape=jax.ShapeDtypeStruct((t, n), out_dtype),
        compiler_params=_params(1),
        name="norm_matmul",
    )(x, gain, w)


def _in_proj_kernel(x_ref, g_ref, w_ref, wgate_ref, wab_ref, cs_ref, cg_ref, cos_ref, sin_ref, attn_ref, yc_ref,
                    gqkv_ref, gate_ref, ab_ref, win_c_ref, halo_c_ref, win_g_ref, halo_g_ref, *,
                    tiles_per_seq, col_chunk):
    i = pl.program_id(0)
    tm = x_ref.shape[0]
    cw = CONV_WIDTH

    @pl.when(i == 0)
    def _():
        halo_c_ref[...] = jnp.zeros_like(halo_c_ref)
        halo_g_ref[...] = jnp.zeros_like(halo_g_ref)

    first = (i % tiles_per_seq) == 0
    xn = _rms(x_ref[...], g_ref[...]).astype(BF16)

    def store_to(ref, lo):
        def epilogue(p, c0, c1):
            ref[:, c0 - lo:c1 - lo] = p
        return epilogue

    lane = lax.broadcasted_iota(jnp.int32, (1, LANES), 1)
    low_half = (lane % ATTN_HEAD_DIM) < ROPE_DIM // 2

    def attn_planes(p, c0, c1):
        group = c0 // ATTN_WIDTH
        for pln in range(ATTN_PLANES):
            x = p[:, pln * LANES:(pln + 1) * LANES]
            if group < 2:
                partner = jnp.where(low_half, pltpu.roll(x, LANES - ROPE_DIM // 2, 1), pltpu.roll(x, ROPE_DIM // 2, 1))
                x = x * cos_ref[...] + partner * sin_ref[...]
            if group == 0:
                x = x * (ATTN_HEAD_DIM ** -0.5)
            attn_ref[group * ATTN_PLANES + pln] = x

    def short_conv(pb, pc, px, c0, c1):
        z = pc * px
        win_c_ref[0:HALO, :] = jnp.where(first, 0.0, halo_c_ref[...])
        win_c_ref[HALO:HALO + tm, :] = z
        halo_c_ref[...] = z[tm - HALO:tm, :]
        y = sum(cs_ref[j:j + 1, :] * win_c_ref[pl.ds(HALO - (CONV_K - 1) + j, tm), :] for j in range(CONV_K))
        yc_ref[...] = (pb * y).astype(yc_ref.dtype)

    gdn0 = 3 * ATTN_WIDTH + 3 * cw

    def gdn_conv(p, c0, c1):
        lo, hi = c0 - gdn0, c1 - gdn0
        win_g_ref[0:HALO, lo:hi] = jnp.where(first, 0.0, halo_g_ref[:, lo:hi])
        win_g_ref[HALO:HALO + tm, lo:hi] = p
        halo_g_ref[:, lo:hi] = p[tm - HALO:tm, :]
        y = sum(cg_ref[j:j + 1, lo:hi] * win_g_ref[pl.ds(HALO - (GDN_CONV_K - 1) + j, tm), lo:hi]
                for j in range(GDN_CONV_K))
        gqkv_ref[:, lo:hi] = _silu(y)

    def chunks(lo, hi, width=col_chunk):
        return [(c0, min(c0 + width, hi)) for c0 in range(lo, hi, width)]

    conv0 = 3 * ATTN_WIDTH
    gate0 = gdn0 + 3 * GDN_WIDTH
    tasks = ([(w_ref, [rng], attn_planes) for rng in chunks(0, conv0, ATTN_WIDTH)]
             + [(w_ref, chunks(conv0, gdn0, cw), short_conv)]
             + [(w_ref, [rng], gdn_conv) for rng in chunks(gdn0, gate0)]
             + [(wgate_ref, [(0, GDN_WIDTH)], store_to(gate_ref, 0)), (wab_ref, [(0, LANES)], store_to(ab_ref, 0))])
    for weights, ranges, epilogue in tasks:
        epilogue(*[_dot(xn, weights[:, c0:c1]) for c0, c1 in ranges], ranges[0][0], ranges[-1][1])


def in_proj(x, gain, w, w_gate, w_ab, conv_short, conv_gdn, cos_t, sin_t, layer, seq, row_tile=512, col_chunk=512):
    t, d = x.shape
    assert seq % row_tile == 0 and (3 * GDN_WIDTH) % col_chunk == 0
    rows = lambda width: pl.BlockSpec((row_tile, width), lambda i: (i, 0))
    widths = (CONV_WIDTH, 3 * GDN_WIDTH, GDN_WIDTH, LANES)
    dtypes = (BF16, F32, F32, F32)
    n_planes = 3 * ATTN_PLANES
    return pl.pallas_call(
        functools.partial(_in_proj_kernel, tiles_per_seq=seq // row_tile, col_chunk=col_chunk),
        grid=(t // row_tile,),
        in_specs=[rows(d), _layer_block(gain, layer), _layer_block(w, layer), _layer_block(w_gate, layer),
                  _layer_block(w_ab, layer), _layer_block(conv_short, layer), _layer_block(conv_gdn, layer),
                  rows(LANES), rows(LANES)],
        out_specs=[pl.BlockSpec((n_planes, row_tile, LANES), lambda i: (0, i, 0))] + [rows(wd) for wd in widths],
        out_shape=[jax.ShapeDtypeStruct((n_planes, t, LANES), F32)]
                  + [jax.ShapeDtypeStruct((t, wd), dt) for wd, dt in zip(widths, dtypes)],
        scratch_shapes=[pltpu.VMEM((HALO + row_tile, CONV_WIDTH), F32), pltpu.VMEM((HALO, CONV_WIDTH), F32),
                        pltpu.VMEM((HALO + row_tile, 3 * GDN_WIDTH), F32), pltpu.VMEM((HALO, 3 * GDN_WIDTH), F32)],
        compiler_params=_params(1),
        name="in_proj",
    )(x, gain, w, w_gate, w_ab, conv_short, conv_gdn, cos_t, sin_t)


def _attn_kernel(x_ref, o_ref, op_ref, lse_ref, *, row_chunk, blocks_per_iter):
    assert HEADS_PER_PLANE == 2
    seq = x_ref.shape[1]
    qb = QUERY_BLOCK
    k0, v0 = ATTN_PLANES, 2 * ATTN_PLANES
    lane = lax.broadcasted_iota(jnp.int32, (1, LANES), 1)
    head_masks = [(lane >= ATTN_HEAD_DIM * h) & (lane < ATTN_HEAD_DIM * (h + 1)) for h in range(HEADS_PER_PLANE)]
    qi = lax.broadcasted_iota(jnp.int32, (qb, 2 * qb), 0)
    kj = lax.broadcasted_iota(jnp.int32, (qb, 2 * qb), 1)
    dist = qi + qb - kj

    for p, (window, dil) in enumerate(DILATED_PATTERNS):
        n_back = window // dil
        nb = seq // dil // qb
        band = (dist >= 0) & (dist <= n_back)

        def blocks(it, carry, p=p, dil=dil, nb=nb, band=band):
            dests, masks, qs, ks, vs = [], [], [], [], []
            for u in range(blocks_per_iter):
                blk = it * blocks_per_iter + u
                r = blk // nb
                n = blk % nb
                cur = r + n * (qb * dil)
                prv = r + jnp.maximum(n - 1, 0) * (qb * dil)
                if dil == 1:
                    rows_c, rows_p = pl.ds(pl.multiple_of(cur, qb), qb), pl.ds(pl.multiple_of(prv, qb), qb)
                else:
                    rows_c, rows_p = pl.ds(cur, qb, stride=dil), pl.ds(prv, qb, stride=dil)
                mask = band & ((kj >= qb) | (n > 0))
                for pln in range(ATTN_PLANES):
                    dests.append((pln, rows_c))
                    masks.append(mask)
                    qs.append(x_ref[pln, rows_c, :])
                    ks.append(jnp.concatenate([x_ref[k0 + pln, rows_p, :], x_ref[k0 + pln, rows_c, :]],
                                              axis=0).astype(BF16))
                    vs.append(jnp.concatenate([x_ref[v0 + pln, rows_p, :], x_ref[v0 + pln, rows_c, :]],
                                              axis=0).astype(BF16))
            heads = range(HEADS_PER_PLANE)
            qh = [[jnp.where(head_masks[h], q, 0.0).astype(BF16) for h in heads] for q in qs]
            s = [[jnp.where(mask, _dot_nt(x, k), -jnp.inf) for x in xs] for xs, k, mask in zip(qh, ks, masks)]
            m = [[jnp.max(x, axis=-1, keepdims=True) for x in xs] for xs in s]
            pexp = [[jnp.exp(x - y) for x, y in zip(xs, ys)] for xs, ys in zip(s, m)]
            l = [[jnp.sum(x, axis=-1, keepdims=True) for x in xs] for xs in pexp]
            oh = [[_dot(x.astype(BF16), v) for x in xs] for xs, v in zip(pexp, vs)]
            for (pln, rows_c), os, ls, ms in zip(dests, oh, l, m):
                op_ref[p, pln, rows_c, :] = jnp.where(head_masks[0], os[0] / ls[0], os[1] / ls[1])
                lse_ref[p, pln, rows_c, :] = jnp.where(head_masks[0], ms[0] + jnp.log(ls[0]), ms[1] + jnp.log(ls[1]))
            return carry

        lax.fori_loop(0, dil * nb // blocks_per_iter, blocks, 0)

    def combine(i, carry):
        rows = pl.ds(pl.multiple_of(i * row_chunk, row_chunk), row_chunk)
        for pln in range(ATTN_PLANES):
            lses = [lse_ref[p, pln, rows, :] for p in range(len(DILATED_PATTERNS))]
            m = functools.reduce(jnp.maximum, lses)
            es = [jnp.exp(x - m) for x in lses]
            num = sum(e * op_ref[p, pln, rows, :] for p, e in enumerate(es))
            o_ref[0, rows, pln * LANES:(pln + 1) * LANES] = (num / sum(es)).astype(o_ref.dtype)
        return carry

    lax.fori_loop(0, seq // row_chunk, combine, 0)


def dilated_attention(planes, batch):
    n_planes, t, _ = planes.shape
    s = t // batch
    assert all(s % (QUERY_BLOCK * d) == 0 and w // d <= QUERY_BLOCK for w, d in DILATED_PATTERNS)
    stats = pltpu.VMEM((len(DILATED_PATTERNS), ATTN_PLANES, s, LANES), F32)
    return pl.pallas_call(
        functools.partial(_attn_kernel, row_chunk=256, blocks_per_iter=2),
        grid=(batch,),
        in_specs=[pl.BlockSpec((n_planes, s, LANES), lambda i: (0, i, 0))],
        out_specs=pl.BlockSpec((1, s, ATTN_WIDTH), lambda i: (i, 0, 0)),
        out_shape=jax.ShapeDtypeStruct((batch, s, ATTN_WIDTH), BF16),
        scratch_shapes=[stats, stats],
        compiler_params=_params(1),
        name="dilated_attention",
    )(planes)


def _mm(a16, b16):
    return jnp.dot(a16, b16, preferred_element_type=F32)


def _lane_sum(x, ones16):
    hi = x.astype(BF16)
    lo = (x - hi.astype(F32)).astype(BF16)
    return _mm(hi, ones16) + _mm(lo, ones16)


def _unit_lower_inverse_minus_eye(a_list, diag_blocks):
    c = a_list[0].shape[0]
    assert c // GDN_SUB == 4
    ds = [jnp.where(diag_blocks, a, 0.0) for a in a_list]
    es = [a - d for a, d in zip(a_list, ds)]
    ns = [-d for d in ds]
    d16 = [d.astype(BF16) for d in ds]
    ps = [_mm(d, d) for d in d16]
    yield
    n_factors = GDN_SUB.bit_length() - 2
    for f in range(n_factors):
        p16 = [p.astype(BF16) for p in ps]
        if f + 1 < n_factors:
            both = [_mm(jnp.concatenate([n.astype(BF16), p], axis=0), p) for n, p in zip(ns, p16)]
            ns = [n + p + b[:c] for n, p, b in zip(ns, ps, both)]
            ps = [b[c:] for b in both]
        else:
            ns = [n + p + _mm(n.astype(BF16), q) for n, p, q in zip(ns, ps, p16)]
        yield
    e16 = [e.astype(BF16) for e in es]
    ms = [e + _mm(n.astype(BF16), q) for e, n, q in zip(es, ns, e16)]
    yield
    m16 = [m.astype(BF16) for m in ms]
    m2s = [_mm(m, m) for m in m16]
    yield
    qs = [m2 - m - _mm(q, m2.astype(BF16)) for m, m2, q in zip(ms, m2s, m16)]
    yield
    return [q + n + _mm(q.astype(BF16), n.astype(BF16)) for q, n in zip(qs, ns)]


def _gdn_kernel(qkv_ref, gate_ref, ab_ref, alog_ref, dtb_ref, nw_ref, o_ref, state_ref, u_ref, wq_ref, ak_ref,
                cd_ref, *, group):
    seq = qkv_ref.shape[1]
    c, dh, nh = GDN_CHUNK, GDN_HEAD_DIM, GDN_HEADS
    rows_g = group * c
    row = lax.broadcasted_iota(jnp.int32, (c, c), 0)
    col = lax.broadcasted_iota(jnp.int32, (c, c), 1)
    causal = row >= col
    strict = row > col
    diag_blocks = (row // GDN_SUB) == (col // GDN_SUB)
    grow = lax.broadcasted_iota(jnp.int32, (rows_g, rows_g), 0)
    gcol = lax.broadcasted_iota(jnp.int32, (rows_g, rows_g), 1)
    tril_group = ((grow >= gcol) & (grow // c == gcol // c)).astype(BF16)
    ones16 = jnp.ones((dh, dh), BF16)
    state_ref[...] = jnp.zeros_like(state_ref)
    units = [(j, h) for j in range(group) for h in range(nh)]

    def chunk_local(gi):
        base = pl.multiple_of(gi * rows_g, rows_g)
        rows = pl.ds(base, rows_g)
        xc = qkv_ref[0, rows, :]

        def l2_normalised(tile, scale):
            x = xc[:, tile * dh:(tile + 1) * dh]
            return x * (lax.rsqrt(_lane_sum(x * x, ones16) + EPS) * scale)

        qn = [l2_normalised(h, dh ** -0.5) for h in range(nh)]
        kn = [l2_normalised(nh + h, 1.0) for h in range(nh)]

        ab = ab_ref[0, rows, :]
        z = ab + dtb_ref[...]
        softplus = jnp.maximum(z, 0.0) + jnp.log1p(jnp.exp(-jnp.abs(z)))
        g_all = -jnp.exp(alog_ref[...]) * softplus
        beta_all = 1.0 / (1.0 + jnp.exp(-ab))
        g_hi = g_all.astype(BF16)
        g_r = g_all - g_hi.astype(F32)
        g_mid = g_r.astype(BF16)
        g_lo = (g_r - g_mid.astype(F32)).astype(BF16)
        decay_all = _mm(tril_group, g_hi) + _mm(tril_group, g_mid) + _mm(tril_group, g_lo)
        decay_t = decay_all.T
        beta_t = beta_all.T

        def lane_broadcast(row_vec):
            return jnp.broadcast_to(row_vec, (dh, c)).T

        qs, ks, vbs, kbs, rels, dcols, edecs, dlasts = [], [], [], [], [], [], [], []
        for j, h in units:
            r0 = j * c
            q = qn[h][r0:r0 + c]
            k = kn[h][r0:r0 + c]
            v = xc[r0:r0 + c, 2 * GDN_WIDTH + h * dh:2 * GDN_WIDTH + (h + 1) * dh]
            drow = decay_t[h:h + 1, r0:r0 + c]
            dcol = lane_broadcast(drow)
            beta = lane_broadcast(beta_t[nh + h:nh + h + 1, r0:r0 + c])
            qs.append(q)
            ks.append(k)
            vbs.append(v * beta)
            kbs.append(k * beta)
            rels.append(jnp.exp(jnp.where(causal, dcol[:, :c] - drow, -jnp.inf)))
            dcols.append(dcol)
            edecs.append(jnp.exp(dcol))
            dlasts.append(dcol[c - 1:c, :])

        k16 = [k.astype(BF16) for k in ks]
        kq = [_dot_nt(jnp.concatenate([kb, q], axis=0).astype(BF16), k) for kb, q, k in zip(kbs, qs, k16)]
        yield
        a_list = [jnp.where(strict, x[:c] * rel, 0.0) for x, rel in zip(kq, rels)]
        attn16 = [jnp.where(causal, x[c:] * rel, 0.0).astype(BF16) for x, rel in zip(kq, rels)]
        n_list = yield from _unit_lower_inverse_minus_eye(a_list, diag_blocks)
        yield
        rhs = [jnp.concatenate([vb, kb * ed], axis=1) for vb, kb, ed in zip(vbs, kbs, edecs)]
        sol = [r + _mm(n.astype(BF16), r.astype(BF16)) for r, n in zip(rhs, n_list)]
        yield
        for i, (x, q, k, at, ed, dl, dc) in enumerate(zip(sol, qs, ks, attn16, edecs, dlasts, dcols)):
            u_ref[i] = x[:, :dh]
            wq_ref[i] = jnp.concatenate([x[:, dh:], q * ed], axis=0).astype(BF16)
            ak_ref[i] = jnp.concatenate([at, (k * jnp.exp(dl - dc)).T.astype(BF16)], axis=0)
            cd_ref[i] = jnp.exp(dl)

    def recurrent(gi):
        base = pl.multiple_of(gi * rows_g, rows_g)
        n_units = len(units)
        us = [u_ref[i] for i in range(n_units)]
        wq16 = [wq_ref[i] for i in range(n_units)]
        ak16 = [ak_ref[i] for i in range(n_units)]
        cds = [cd_ref[i] for i in range(n_units)]
        states = [state_ref[h] for h in range(nh)]
        for j in range(group):
            idx = [j * nh + h for h in range(nh)]
            ws_qs = [_mm(wq16[i], states[h].astype(BF16)) for h, i in enumerate(idx)]
            yield
            v16 = [(us[i] - x[:c]).astype(BF16) for i, x in zip(idx, ws_qs)]
            av_kv = [_mm(ak16[i], v) for i, v in zip(idx, v16)]
            yield
            states = [s * cds[i] + y[c:] for s, i, y in zip(states, idx, av_kv)]
            out_rows = pl.ds(base + j * c, c)
            for h in range(nh):
                o = ws_qs[h][c:] + av_kv[h][:c]
                gate = gate_ref[0, out_rows, h * dh:(h + 1) * dh]
                o_ref[0, out_rows, h * dh:(h + 1) * dh] = (_rms(o, nw_ref[...]) * _silu(gate)).astype(o_ref.dtype)
        for h in range(nh):
            state_ref[h] = states[h]

    def run(*gens):
        gens = list(gens)
        while gens:
            for g in list(gens):
                try:
                    next(g)
                except StopIteration:
                    gens.remove(g)

    def body(gi, carry):
        run(recurrent(gi - 1), chunk_local(gi))
        return carry

    n_groups = seq // rows_g
    run(chunk_local(0))
    lax.fori_loop(1, n_groups, body, 0)
    run(recurrent(n_groups - 1))


def gated_deltanet(qkv, gate, ab, alog_v, dtb_v, norm_w, layer, group=4):
    b, s, _ = qkv.shape
    n_units = group * GDN_HEADS
    seq_block = lambda width: pl.BlockSpec((1, s, width), lambda i: (i, 0, 0))
    return pl.pallas_call(
        functools.partial(_gdn_kernel, group=group),
        grid=(b,),
        in_specs=[seq_block(3 * GDN_WIDTH), seq_block(GDN_WIDTH), seq_block(LANES),
                  _layer_block(alog_v, layer), _layer_block(dtb_v, layer), _layer_block(norm_w, layer)],
        out_specs=seq_block(GDN_WIDTH),
        out_shape=jax.ShapeDtypeStruct((b, s, GDN_WIDTH), BF16),
        scratch_shapes=[pltpu.VMEM((GDN_HEADS, GDN_HEAD_DIM, GDN_HEAD_DIM), F32),
                        pltpu.VMEM((n_units, GDN_CHUNK, GDN_HEAD_DIM), F32),
                        pltpu.VMEM((n_units, 2 * GDN_CHUNK, GDN_HEAD_DIM), BF16),
                        pltpu.VMEM((n_units, GDN_CHUNK + GDN_HEAD_DIM, GDN_CHUNK), BF16),
                        pltpu.VMEM((n_units, 1, GDN_HEAD_DIM), F32)],
        compiler_params=_params(1),
        name="gated_deltanet",
    )(qkv, gate, ab, alog_v, dtb_v, norm_w)


def _post_mixer_kernel(ya_ref, yc_ref, yg_ref, h_ref, kv_ref, wout_ref, wq_ref, wo_ref, wgu_ref, wd_ref,
                       g_mix_ref, g_xpre_ref, g_xpost_ref, g_fpre_ref, g_fpost_ref, o_ref, acc_ref, *, hid_chunk):
    h = h_ref[0]
    d = h.shape[-1]
    a_w, c_w = ya_ref.shape[2], yc_ref.shape[2]
    mix = (_dot(ya_ref[0], wout_ref[0:a_w, :]) + _dot(yc_ref[0], wout_ref[a_w:a_w + c_w, :])
           + _dot(yg_ref[0], wout_ref[a_w + c_w:, :]))
    h = h + _rms(mix, g_mix_ref[...])
    dh = d // XATTN_HEADS
    q = _dot(_rms(h, g_xpre_ref[...]).astype(BF16), wq_ref[...]).astype(BF16)
    heads = []
    for i in range(XATTN_HEADS):
        k = kv_ref[0, :, i * dh:(i + 1) * dh]
        v = kv_ref[0, :, d + i * dh:d + (i + 1) * dh]
        s = _dot_nt(q[:, i * dh:(i + 1) * dh], k) * (dh ** -0.5)
        p = jnp.exp(s - jnp.max(s, axis=-1, keepdims=True))
        l = jnp.sum(p, axis=-1, keepdims=True)
        heads.append((_dot(p.astype(BF16), v) / l).astype(BF16))
    h = h + _rms(_dot(jnp.concatenate(heads, axis=-1), wo_ref[...]), g_xpost_ref[...])
    hidden = wd_ref.shape[0]
    hn = _rms(h, g_fpre_ref[...]).astype(BF16)
    for idx, c in enumerate(range(0, hidden, hid_chunk)):
        e = min(c + hid_chunk, hidden)
        gate = _dot(hn, wgu_ref[:, c:e])
        up = _dot(hn, wgu_ref[:, hidden + c:hidden + e])
        part = _dot((_silu(gate) * up).astype(BF16), wd_ref[c:e, :])
        if idx == 0:
            acc_ref[...] = part
        else:
            acc_ref[...] += part
    o_ref[0] = h + _rms(acc_ref[...], g_fpost_ref[...])


def post_mixer(ya, yc, yg, h, kv, weights, gains, layer, row_tile=512, hid_chunk=512):
    b, s, d = h.shape
    m = kv.shape[1]
    tile = lambda width: pl.BlockSpec((1, row_tile, width), lambda i, j: (i, j, 0))
    return pl.pallas_call(
        functools.partial(_post_mixer_kernel, hid_chunk=hid_chunk),
        grid=(b, s // row_tile),
        in_specs=[tile(ya.shape[2]), tile(yc.shape[2]), tile(yg.shape[2]), tile(d),
                  pl.BlockSpec((1, m, 2 * d), lambda i, j: (i, 0, 0))]
                 + [_layer_block(w, layer) for w in weights] + [_layer_block(g, layer) for g in gains],
        out_specs=tile(d),
        out_shape=jax.ShapeDtypeStruct((b, s, d), F32),
        scratch_shapes=[pltpu.VMEM((row_tile, d), F32)],
        compiler_params=_params(2),
        name="post_mixer",
    )(ya, yc, yg, h, kv, *weights, *gains)


def _rotary_tables(positions):
    inv_freq = jnp.float32(ROPE_THETA) ** (-jnp.arange(0, ROPE_DIM, 2, dtype=F32) / ROPE_DIM)
    ang = positions.astype(F32)[..., None] * inv_freq
    cos, sin = jnp.cos(ang), jnp.sin(ang)
    rest = ATTN_HEAD_DIM - ROPE_DIM
    ones = jnp.ones(ang.shape[:-1] + (rest,), F32)
    zeros = jnp.zeros(ang.shape[:-1] + (rest,), F32)
    reps = LANES // ATTN_HEAD_DIM
    tile = lambda parts: jnp.tile(jnp.concatenate(parts, axis=-1), (1, 1, reps))
    return tile([cos, cos, ones]), tile([-sin, sin, zeros])


def _split_w_in(w):
    ab0 = IN_MAIN - GDN_WIDTH
    ab1 = ab0 + 2 * GDN_HEADS
    w = w.astype(BF16)
    return w, w[..., ab1:], jnp.pad(w[..., ab0:ab1], ((0, 0), (0, 0), (0, LANES - 2 * GDN_HEADS)))


def _lane_padded(v):
    return jnp.pad(v.astype(F32), ((0, 0), (0, LANES - v.shape[1])))[:, None, :]


def kernel(x, mem, positions, norm_mix_pre, norm_mix_post, w_in, conv_short, conv_gdn, gdn_a_log, gdn_dt_bias, gdn_norm, w_out, norm_mem, norm_xattn_pre, norm_xattn_post, w_xq, w_xkv, w_xo, norm_ffn_pre, norm_ffn_post, w_gate_up, w_down):
    b, s, d = x.shape
    m = mem.shape[1]
    depth = w_in.shape[0]
    cos_t, sin_t = (t.reshape(b * s, LANES) for t in _rotary_tables(positions))
    row = lambda g: g[:, None, :]
    (w_in16, w_gate16, w_ab16), w_xkv16 = _split_w_in(w_in), w_xkv.astype(BF16)
    alog_v, dtb_v = _lane_padded(gdn_a_log), _lane_padded(gdn_dt_bias)
    mem2 = mem.reshape(b * m, d)
    weights = (w_out.astype(BF16), w_xq.astype(BF16), w_xo.astype(BF16), w_gate_up.astype(BF16), w_down.astype(BF16))
    gains = tuple(row(g) for g in (norm_mix_post, norm_xattn_pre, norm_xattn_post, norm_ffn_pre, norm_ffn_post))
    h = x
    for l in range(depth):
        planes, yc, qkv_g, gate, ab = in_proj(h.reshape(b * s, d), row(norm_mix_pre), w_in16, w_gate16, w_ab16,
                                              conv_short, conv_gdn, cos_t, sin_t, l, s)
        ya = dilated_attention(planes, b)
        yg = gated_deltanet(qkv_g.reshape(b, s, -1), gate.reshape(b, s, -1), ab.reshape(b, s, -1),
                            alog_v, dtb_v, row(gdn_norm), l)
        kv = norm_matmul(mem2, row(norm_mem), w_xkv16, l, out_dtype=BF16).reshape(b, m, 2 * d)
        h = post_mixer(ya, yc.reshape(b, s, -1), yg, h, kv, weights, gains, l)
    return h
```

```python
import functools

import jax
import jax.numpy as jnp
from jax import lax
from jax.experimental import pallas as pl
from jax.experimental.pallas import tpu as pltpu

F32 = jnp.float32
BF16 = jnp.bfloat16
EPS = 1e-6

ATTN_HEADS = 4
ATTN_HEAD_DIM = 64
ATTN_WIDTH = ATTN_HEADS * ATTN_HEAD_DIM
DILATED_PATTERNS = ((128, 1), (512, 4), (2048, 16))
QUERY_BLOCK = 128
ROPE_THETA = 500000.0
ROPE_DIM = ATTN_HEAD_DIM // 4
CONV_WIDTH = 256
CONV_K = 3
GDN_HEADS = 4
GDN_HEAD_DIM = 128
GDN_WIDTH = GDN_HEADS * GDN_HEAD_DIM
GDN_CONV_K = 4
GDN_CHUNK = 64
GDN_SUB = 16
XATTN_HEADS = 4
LANES = 128
ATTN_PLANES = ATTN_WIDTH // LANES
HEADS_PER_PLANE = LANES // ATTN_HEAD_DIM
HALO = 8
IN_MAIN = 3 * ATTN_WIDTH + 3 * CONV_WIDTH + 4 * GDN_WIDTH
VMEM_LIMIT = 56 * 1024 * 1024


def _params(n_grid):
    return pltpu.CompilerParams(dimension_semantics=("arbitrary",) * n_grid,
                                vmem_limit_bytes=VMEM_LIMIT)


def _rms(x, gain):
    return x * lax.rsqrt(jnp.mean(x * x, axis=-1, keepdims=True) + EPS) * gain


def _dot(a, b):
    return jnp.dot(a, b, preferred_element_type=F32)


def _dot_nt(a, b):
    return lax.dot_general(a, b, (((1,), (1,)), ((), ())), preferred_element_type=F32)


def _silu(x):
    return x * (1.0 / (1.0 + jnp.exp(-x)))


def _layer_block(stacked, layer):
    tail = stacked.shape[1:]
    return pl.BlockSpec((None,) + tail, lambda *_: (layer,) + (0,) * len(tail), pipeline_mode=pl.Buffered(1))


def _norm_matmul_kernel(x_ref, g_ref, w_ref, o_ref, *, col_chunk):
    xn = _rms(x_ref[...], g_ref[...]).astype(BF16)
    n = w_ref.shape[1]
    for c in range(0, n, col_chunk):
        e = min(c + col_chunk, n)
        o_ref[:, c:e] = _dot(xn, w_ref[:, c:e]).astype(o_ref.dtype)


def norm_matmul(x, gain, w, layer, *, out_dtype, row_tile=512, col_chunk=512):
    t, d = x.shape
    n = w.shape[2]
    return pl.pallas_call(
        functools.partial(_norm_matmul_kernel, col_chunk=col_chunk),
        grid=(t // row_tile,),
        in_specs=[pl.BlockSpec((row_tile, d), lambda i: (i, 0)), _layer_block(gain, layer), _layer_block(w, layer)],
        out_specs=pl.BlockSpec((row_tile, n), lambda i: (i, 0)),
        out_shape=jax.ShapeDtypeStruct((t, n), out_dtype),
        compiler_params=_params(1),
        name="norm_matmul",
    )(x, gain, w)


def _in_proj_kernel(x_ref, g_ref, w_ref, wgate_ref, wab_ref, cs_ref, cg_ref, cos_ref, sin_ref, attn_ref, yc_ref,
                    gqkv_ref, gate_ref, ab_ref, win_c_ref, halo_c_ref, win_g_ref, halo_g_ref, *,
                    tiles_per_seq, col_chunk):
    i = pl.program_id(0)
    tm = x_ref.shape[0]
    cw = CONV_WIDTH

    @pl.when(i == 0)
    def _():
        halo_c_ref[...] = jnp.zeros_like(halo_c_ref)
        halo_g_ref[...] = jnp.zeros_like(halo_g_ref)

    first = (i % tiles_per_seq) == 0
    xn = _rms(x_ref[...], g_ref[...]).astype(BF16)

    def store_to(ref, lo):
        def epilogue(p, c0, c1):
            ref[:, c0 - lo:c1 - lo] = p
        return epilogue

    lane = lax.broadcasted_iota(jnp.int32, (1, LANES), 1)
    low_half = (lane % ATTN_HEAD_DIM) < ROPE_DIM // 2

    def attn_planes(p, c0, c1):
        group = c0 // ATTN_WIDTH
        for pln in range(ATTN_PLANES):
            x = p[:, pln * LANES:(pln + 1) * LANES]
            if group < 2:
                partner = jnp.where(low_half, pltpu.roll(x, LANES - ROPE_DIM // 2, 1), pltpu.roll(x, ROPE_DIM // 2, 1))
                x = x * cos_ref[...] + partner * sin_ref[...]
            if group == 0:
                x = x * (ATTN_HEAD_DIM ** -0.5)
            attn_ref[group * ATTN_PLANES + pln] = x

    def short_conv(pb, pc, px, c0, c1):
        z = pc * px
        win_c_ref[0:HALO, :] = jnp.where(first, 0.0, halo_c_ref[...])
        win_c_ref[HALO:HALO + tm, :] = z
        halo_c_ref[...] = z[tm - HALO:tm, :]
        y = sum(cs_ref[j:j + 1, :] * win_c_ref[pl.ds(HALO - (CONV_K - 1) + j, tm), :] for j in range(CONV_K))
        yc_ref[...] = (pb * y).astype(yc_ref.dtype)

    gdn0 = 3 * ATTN_WIDTH + 3 * cw

    def gdn_conv(p, c0, c1):
        lo, hi = c0 - gdn0, c1 - gdn0
        win_g_ref[0:HALO, lo:hi] = jnp.where(first, 0.0, halo_g_ref[:, lo:hi])
        win_g_ref[HALO:HALO + tm, lo:hi] = p
        halo_g_ref[:, lo:hi] = p[tm - HALO:tm, :]
        y = sum(cg_ref[j:j + 1, lo:hi] * win_g_ref[pl.ds(HALO - (GDN_CONV_K - 1) + j, tm), lo:hi]
                for j in range(GDN_CONV_K))
        gqkv_ref[:, lo:hi] = _silu(y)

    def chunks(lo, hi, width=col_chunk):
        return [(c0, min(c0 + width, hi)) for c0 in range(lo, hi, width)]

    conv0 = 3 * ATTN_WIDTH
    gate0 = gdn0 + 3 * GDN_WIDTH
    tasks = ([(w_ref, [rng], attn_planes) for rng in chunks(0, conv0, ATTN_WIDTH)]
             + [(w_ref, chunks(conv0, gdn0, cw), short_conv)]
             + [(w_ref, [rng], gdn_conv) for rng in chunks(gdn0, gate0)]
             + [(wgate_ref, [(0, GDN_WIDTH)], store_to(gate_ref, 0)), (wab_ref, [(0, LANES)], store_to(ab_ref, 0))])
    for weights, ranges, epilogue in tasks:
        epilogue(*[_dot(xn, weights[:, c0:c1]) for c0, c1 in ranges], ranges[0][0], ranges[-1][1])


def in_proj(x, gain, w, w_gate, w_ab, conv_short, conv_gdn, cos_t, sin_t, layer, seq, row_tile=512, col_chunk=512):
    t, d = x.shape
    assert seq % row_tile == 0 and (3 * GDN_WIDTH) % col_chunk == 0
    rows = lambda width: pl.BlockSpec((row_tile, width), lambda i: (i, 0))
    widths = (CONV_WIDTH, 3 * GDN_WIDTH, GDN_WIDTH, LANES)
    dtypes = (BF16, F32, F32, F32)
    n_planes = 3 * ATTN_PLANES
    return pl.pallas_call(
        functools.partial(_in_proj_kernel, tiles_per_seq=seq // row_tile, col_chunk=col_chunk),
        grid=(t // row_tile,),
        in_specs=[rows(d), _layer_block(gain, layer), _layer_block(w, layer), _layer_block(w_gate, layer),
                  _layer_block(w_ab, layer), _layer_block(conv_short, layer), _layer_block(conv_gdn, layer),
                  rows(LANES), rows(LANES)],
        out_specs=[pl.BlockSpec((n_planes, row_tile, LANES), lambda i: (0, i, 0))] + [rows(wd) for wd in widths],
        out_shape=[jax.ShapeDtypeStruct((n_planes, t, LANES), F32)]
                  + [jax.ShapeDtypeStruct((t, wd), dt) for wd, dt in zip(widths, dtypes)],
        scratch_shapes=[pltpu.VMEM((HALO + row_tile, CONV_WIDTH), F32), pltpu.VMEM((HALO, CONV_WIDTH), F32),
                        pltpu.VMEM((HALO + row_tile, 3 * GDN_WIDTH), F32), pltpu.VMEM((HALO, 3 * GDN_WIDTH), F32)],
        compiler_params=_params(1),
        name="in_proj",
    )(x, gain, w, w_gate, w_ab, conv_short, conv_gdn, cos_t, sin_t)


def _attn_kernel(x_ref, o_ref, op_ref, lse_ref, *, row_chunk, blocks_per_iter):
    assert HEADS_PER_PLANE == 2
    seq = x_ref.shape[1]
    qb = QUERY_BLOCK
    k0, v0 = ATTN_PLANES, 2 * ATTN_PLANES
    lane = lax.broadcasted_iota(jnp.int32, (1, LANES), 1)
    head_masks = [(lane >= ATTN_HEAD_DIM * h) & (lane < ATTN_HEAD_DIM * (h + 1)) for h in range(HEADS_PER_PLANE)]
    qi = lax.broadcasted_iota(jnp.int32, (qb, 2 * qb), 0)
    kj = lax.broadcasted_iota(jnp.int32, (qb, 2 * qb), 1)
    dist = qi + qb - kj

    for p, (window, dil) in enumerate(DILATED_PATTERNS):
        n_back = window // dil
        nb = seq // dil // qb
        band = (dist >= 0) & (dist <= n_back)

        def blocks(it, carry, p=p, dil=dil, nb=nb, band=band):
            dests, masks, qs, ks, vs = [], [], [], [], []
            for u in range(blocks_per_iter):
                blk = it * blocks_per_iter + u
                r = blk // nb
                n = blk % nb
                cur = r + n * (qb * dil)
                prv = r + jnp.maximum(n - 1, 0) * (qb * dil)
                if dil == 1:
                    rows_c, rows_p = pl.ds(pl.multiple_of(cur, qb), qb), pl.ds(pl.multiple_of(prv, qb), qb)
                else:
                    rows_c, rows_p = pl.ds(cur, qb, stride=dil), pl.ds(prv, qb, stride=dil)
                mask = band & ((kj >= qb) | (n > 0))
                for pln in range(ATTN_PLANES):
                    dests.append((pln, rows_c))
                    masks.append(mask)
                    qs.append(x_ref[pln, rows_c, :])
                    ks.append(jnp.concatenate([x_ref[k0 + pln, rows_p, :], x_ref[k0 + pln, rows_c, :]],
                                              axis=0).astype(BF16))
                    vs.append(jnp.concatenate([x_ref[v0 + pln, rows_p, :], x_ref[v0 + pln, rows_c, :]],
                                              axis=0).astype(BF16))
            heads = range(HEADS_PER_PLANE)
            qh = [[jnp.where(head_masks[h], q, 0.0).astype(BF16) for h in heads] for q in qs]
            s = [[jnp.where(mask, _dot_nt(x, k), -jnp.inf) for x in xs] for xs, k, mask in zip(qh, ks, masks)]
            m = [[jnp.max(x, axis=-1, keepdims=True) for x in xs] for xs in s]
            pexp = [[jnp.exp(x - y) for x, y in zip(xs, ys)] for xs, ys in zip(s, m)]
            l = [[jnp.sum(x, axis=-1, keepdims=True) for x in xs] for xs in pexp]
            oh = [[_dot(x.astype(BF16), v) for x in xs] for xs, v in zip(pexp, vs)]
            for (pln, rows_c), os, ls, ms in zip(dests, oh, l, m):
                op_ref[p, pln, rows_c, :] = jnp.where(head_masks[0], os[0] / ls[0], os[1] / ls[1])
                lse_ref[p, pln, rows_c, :] = jnp.where(head_masks[0], ms[0] + jnp.log(ls[0]), ms[1] + jnp.log(ls[1]))
            return carry

        lax.fori_loop(0, dil * nb // blocks_per_iter, blocks, 0)

    def combine(i, carry):
        rows = pl.ds(pl.multiple_of(i * row_chunk, row_chunk), row_chunk)
        for pln in range(ATTN_PLANES):
            lses = [lse_ref[p, pln, rows, :] for p in range(len(DILATED_PATTERNS))]
            m = functools.reduce(jnp.maximum, lses)
            es = [jnp.exp(x - m) for x in lses]
            num = sum(e * op_ref[p, pln, rows, :] for p, e in enumerate(es))
            o_ref[0, rows, pln * LANES:(pln + 1) * LANES] = (num / sum(es)).astype(o_ref.dtype)
        return carry

    lax.fori_loop(0, seq // row_chunk, combine, 0)


def dilated_attention(planes, batch):
    n_planes, t, _ = planes.shape
    s = t // batch
    assert all(s % (QUERY_BLOCK * d) == 0 and w // d <= QUERY_BLOCK for w, d in DILATED_PATTERNS)
    stats = pltpu.VMEM((len(DILATED_PATTERNS), ATTN_PLANES, s, LANES), F32)
    return pl.pallas_call(
        functools.partial(_attn_kernel, row_chunk=256, blocks_per_iter=2),
        grid=(batch,),
        in_specs=[pl.BlockSpec((n_planes, s, LANES), lambda i: (0, i, 0))],
        out_specs=pl.BlockSpec((1, s, ATTN_WIDTH), lambda i: (i, 0, 0)),
        out_shape=jax.ShapeDtypeStruct((batch, s, ATTN_WIDTH), BF16),
        scratch_shapes=[stats, stats],
        compiler_params=_params(1),
        name="dilated_attention",
    )(planes)


def _mm(a16, b16):
    return jnp.dot(a16, b16, preferred_element_type=F32)


def _lane_sum(x, ones16):
    hi = x.astype(BF16)
    lo = (x - hi.astype(F32)).astype(BF16)
    return _mm(jnp.concatenate([hi, lo], axis=1), ones16)


def _unit_lower_inverse_minus_eye(a_list, diag_blocks):
    c = a_list[0].shape[0]
    assert c // GDN_SUB == 4
    ds = [jnp.where(diag_blocks, a, 0.0) for a in a_list]
    es = [a - d for a, d in zip(a_list, ds)]
    ns = [-d for d in ds]
    d16 = [d.astype(BF16) for d in ds]
    ps = [_mm(d, d) for d in d16]
    yield
    n_factors = GDN_SUB.bit_length() - 2
    for f in range(n_factors):
        p16 = [p.astype(BF16) for p in ps]
        if f + 1 < n_factors:
            both = [_mm(jnp.concatenate([n.astype(BF16), p], axis=0), p) for n, p in zip(ns, p16)]
            ns = [n + p + b[:c] for n, p, b in zip(ns, ps, both)]
            ps = [b[c:] for b in both]
        else:
            ns = [n + p + _mm(n.astype(BF16), q) for n, p, q in zip(ns, ps, p16)]
        yield
    e16 = [e.astype(BF16) for e in es]
    ms = [e + _mm(n.astype(BF16), q) for e, n, q in zip(es, ns, e16)]
    yield
    m16 = [m.astype(BF16) for m in ms]
    m2s = [_mm(m, m) for m in m16]
    yield
    qs = [m2 - m - _mm(q, m2.astype(BF16)) for m, m2, q in zip(ms, m2s, m16)]
    yield
    return [q + n + _mm(q.astype(BF16), n.astype(BF16)) for q, n in zip(qs, ns)]


def _gdn_kernel(qkv_ref, gate_ref, ab_ref, alog_ref, dtb_ref, nw_ref, o_ref, state_ref, u_ref, wq_ref, ak_ref,
                cd_ref, *, group):
    seq = qkv_ref.shape[1]
    c, dh, nh = GDN_CHUNK, GDN_HEAD_DIM, GDN_HEADS
    rows_g = group * c
    row = lax.broadcasted_iota(jnp.int32, (c, c), 0)
    col = lax.broadcasted_iota(jnp.int32, (c, c), 1)
    causal = row >= col
    strict = row > col
    diag_blocks = (row // GDN_SUB) == (col // GDN_SUB)
    grow = lax.broadcasted_iota(jnp.int32, (rows_g, rows_g), 0)
    gcol = lax.broadcasted_iota(jnp.int32, (rows_g, rows_g), 1)
    tril_group = ((grow >= gcol) & (grow // c == gcol // c)).astype(BF16)
    ones16 = jnp.ones((2 * dh, dh), BF16)
    state_ref[...] = jnp.zeros_like(state_ref)
    units = [(j, h) for j in range(group) for h in range(nh)]

    def chunk_local(gi):
        base = pl.multiple_of(gi * rows_g, rows_g)
        rows = pl.ds(base, rows_g)
        xc = qkv_ref[0, rows, :]

        def l2_normalised(tile, scale):
            x = xc[:, tile * dh:(tile + 1) * dh]
            return x * (lax.rsqrt(_lane_sum(x * x, ones16) + EPS) * scale)

        qn = [l2_normalised(h, dh ** -0.5) for h in range(nh)]
        kn = [l2_normalised(nh + h, 1.0) for h in range(nh)]

        ab = ab_ref[0, rows, :]
        z = ab + dtb_ref[...]
        softplus = jnp.maximum(z, 0.0) + jnp.log1p(jnp.exp(-jnp.abs(z)))
        g_all = -jnp.exp(alog_ref[...]) * softplus
        beta_all = 1.0 / (1.0 + jnp.exp(-ab))
        g_hi = g_all.astype(BF16)
        g_r = g_all - g_hi.astype(F32)
        g_mid = g_r.astype(BF16)
        g_lo = (g_r - g_mid.astype(F32)).astype(BF16)
        decay_all = _mm(tril_group, g_hi) + _mm(tril_group, g_mid) + _mm(tril_group, g_lo)
        decay_t = decay_all.T
        beta_t = beta_all.T

        def lane_broadcast(row_vec):
            return jnp.broadcast_to(row_vec, (dh, c)).T

        qs, ks, vbs, kbs, rels, dcols, edecs, dlasts = [], [], [], [], [], [], [], []
        for j, h in units:
            r0 = j * c
            q = qn[h][r0:r0 + c]
            k = kn[h][r0:r0 + c]
            v = xc[r0:r0 + c, 2 * GDN_WIDTH + h * dh:2 * GDN_WIDTH + (h + 1) * dh]
            drow = decay_t[h:h + 1, r0:r0 + c]
            dcol = lane_broadcast(drow)
            beta = lane_broadcast(beta_t[nh + h:nh + h + 1, r0:r0 + c])
            qs.append(q)
            ks.append(k)
            vbs.append(v * beta)
            kbs.append(k * beta)
            rels.append(jnp.exp(jnp.where(causal, dcol[:, :c] - drow, -jnp.inf)))
            dcols.append(dcol)
            edecs.append(jnp.exp(dcol))
            dlasts.append(dcol[c - 1:c, :])

        k16 = [k.astype(BF16) for k in ks]
        kq = [_dot_nt(jnp.concatenate([kb, q], axis=0).astype(BF16), k) for kb, q, k in zip(kbs, qs, k16)]
        yield
        a_list = [jnp.where(strict, x[:c] * rel, 0.0) for x, rel in zip(kq, rels)]
        attn16 = [jnp.where(causal, x[c:] * rel, 0.0).astype(BF16) for x, rel in zip(kq, rels)]
        n_list = yield from _unit_lower_inverse_minus_eye(a_list, diag_blocks)
        yield
        rhs = [jnp.concatenate([vb, kb * ed], axis=1) for vb, kb, ed in zip(vbs, kbs, edecs)]
        sol = [r + _mm(n.astype(BF16), r.astype(BF16)) for r, n in zip(rhs, n_list)]
        yield
        for i, (x, q, k, at, ed, dl, dc) in enumerate(zip(sol, qs, ks, attn16, edecs, dlasts, dcols)):
            u_ref[i] = x[:, :dh]
            wq_ref[i] = jnp.concatenate([x[:, dh:], q * ed], axis=0).astype(BF16)
            ak_ref[i] = jnp.concatenate([at, (k * jnp.exp(dl - dc)).T.astype(BF16)], axis=0)
            cd_ref[i] = jnp.exp(dl)

    def recurrent(gi):
        base = pl.multiple_of(gi * rows_g, rows_g)
        n_units = len(units)
        us = [u_ref[i] for i in range(n_units)]
        wq16 = [wq_ref[i] for i in range(n_units)]
        ak16 = [ak_ref[i] for i in range(n_units)]
        cds = [cd_ref[i] for i in range(n_units)]
        states = [state_ref[h] for h in range(nh)]
        for j in range(group):
            idx = [j * nh + h for h in range(nh)]
            ws_qs = [_mm(wq16[i], states[h].astype(BF16)) for h, i in enumerate(idx)]
            yield
            v16 = [(us[i] - x[:c]).astype(BF16) for i, x in zip(idx, ws_qs)]
            av_kv = [_mm(ak16[i], v) for i, v in zip(idx, v16)]
            yield
            states = [s * cds[i] + y[c:] for s, i, y in zip(states, idx, av_kv)]
            out_rows = pl.ds(base + j * c, c)
            for h in range(nh):
                o = ws_qs[h][c:] + av_kv[h][:c]
                gate = gate_ref[0, out_rows, h * dh:(h + 1) * dh]
                o_ref[0, out_rows, h * dh:(h + 1) * dh] = (_rms(o, nw_ref[...]) * _silu(gate)).astype(o_ref.dtype)
        for h in range(nh):
            state_ref[h] = states[h]

    def run(*gens):
        gens = list(gens)
        while gens:
            for g in list(gens):
                try:
                    next(g)
                except StopIteration:
                    gens.remove(g)

    def body(gi, carry):
        run(recurrent(gi - 1), chunk_local(gi))
        return carry

    n_groups = seq // rows_g
    run(chunk_local(0))
    lax.fori_loop(1, n_groups, body, 0)
    run(recurrent(n_groups - 1))


def gated_deltanet(qkv, gate, ab, alog_v, dtb_v, norm_w, layer, group=4):
    b, s, _ = qkv.shape
    n_units = group * GDN_HEADS
    seq_block = lambda width: pl.BlockSpec((1, s, width), lambda i: (i, 0, 0))
    return pl.pallas_call(
        functools.partial(_gdn_kernel, group=group),
        grid=(b,),
        in_specs=[seq_block(3 * GDN_WIDTH), seq_block(GDN_WIDTH), seq_block(LANES),
                  _layer_block(alog_v, layer), _layer_block(dtb_v, layer), _layer_block(norm_w, layer)],
        out_specs=seq_block(GDN_WIDTH),
        out_shape=jax.ShapeDtypeStruct((b, s, GDN_WIDTH), BF16),
        scratch_shapes=[pltpu.VMEM((GDN_HEADS, GDN_HEAD_DIM, GDN_HEAD_DIM), F32),
                        pltpu.VMEM((n_units, GDN_CHUNK, GDN_HEAD_DIM), F32),
                        pltpu.VMEM((n_units, 2 * GDN_CHUNK, GDN_HEAD_DIM), BF16),
                        pltpu.VMEM((n_units, GDN_CHUNK + GDN_HEAD_DIM, GDN_CHUNK), BF16),
                        pltpu.VMEM((n_units, 1, GDN_HEAD_DIM), F32)],
        compiler_params=_params(1),
        name="gated_deltanet",
    )(qkv, gate, ab, alog_v, dtb_v, norm_w)


def _post_mixer_kernel(ya_ref, yc_ref, yg_ref, h_ref, kv_ref, wout_ref, wq_ref, wo_ref, wgu_ref, wd_ref,
                       g_mix_ref, g_xpre_ref, g_xpost_ref, g_fpre_ref, g_fpost_ref, o_ref, acc_ref, *, hid_chunk):
    h = h_ref[0]
    d = h.shape[-1]
    a_w, c_w = ya_ref.shape[2], yc_ref.shape[2]
    mix = (_dot(ya_ref[0], wout_ref[0:a_w, :]) + _dot(yc_ref[0], wout_ref[a_w:a_w + c_w, :])
           + _dot(yg_ref[0], wout_ref[a_w + c_w:, :]))
    h = h + _rms(mix, g_mix_ref[...])
    dh = d // XATTN_HEADS
    q = _dot(_rms(h, g_xpre_ref[...]).astype(BF16), wq_ref[...]).astype(BF16)
    heads = []
    for i in range(XATTN_HEADS):
        k = kv_ref[0, :, i * dh:(i + 1) * dh]
        v = kv_ref[0, :, d + i * dh:d + (i + 1) * dh]
        s = _dot_nt(q[:, i * dh:(i + 1) * dh], k) * (dh ** -0.5)
        p = jnp.exp(s - jnp.max(s, axis=-1, keepdims=True))
        l = jnp.sum(p, axis=-1, keepdims=True)
        heads.append((_dot(p.astype(BF16), v) / l).astype(BF16))
    h = h + _rms(_dot(jnp.concatenate(heads, axis=-1), wo_ref[...]), g_xpost_ref[...])
    hidden = wd_ref.shape[0]
    hn = _rms(h, g_fpre_ref[...]).astype(BF16)
    for idx, c in enumerate(range(0, hidden, hid_chunk)):
        e = min(c + hid_chunk, hidden)
        gate = _dot(hn, wgu_ref[:, c:e])
        up = _dot(hn, wgu_ref[:, hidden + c:hidden + e])
        part = _dot((_silu(gate) * up).astype(BF16), wd_ref[c:e, :])
        if idx == 0:
            acc_ref[...] = part
        else:
            acc_ref[...] += part
    o_ref[0] = h + _rms(acc_ref[...], g_fpost_ref[...])


def post_mixer(ya, yc, yg, h, kv, weights, gains, layer, row_tile=512, hid_chunk=512):
    b, s, d = h.shape
    m = kv.shape[1]
    tile = lambda width: pl.BlockSpec((1, row_tile, width), lambda i, j: (i, j, 0))
    return pl.pallas_call(
        functools.partial(_post_mixer_kernel, hid_chunk=hid_chunk),
        grid=(b, s // row_tile),
        in_specs=[tile(ya.shape[2]), tile(yc.shape[2]), tile(yg.shape[2]), tile(d),
                  pl.BlockSpec((1, m, 2 * d), lambda i, j: (i, 0, 0))]
                 + [_layer_block(w, layer) for w in weights] + [_layer_block(g, layer) for g in gains],
        out_specs=tile(d),
        out_shape=jax.ShapeDtypeStruct((b, s, d), F32),
        scratch_shapes=[pltpu.VMEM((row_tile, d), F32)],
        compiler_params=_params(2),
        name="post_mixer",
    )(ya, yc, yg, h, kv, *weights, *gains)


def _rotary_tables(positions):
    inv_freq = jnp.float32(ROPE_THETA) ** (-jnp.arange(0, ROPE_DIM, 2, dtype=F32) / ROPE_DIM)
    ang = positions.astype(F32)[..., None] * inv_freq
    cos, sin = jnp.cos(ang), jnp.sin(ang)
    rest = ATTN_HEAD_DIM - ROPE_DIM
    ones = jnp.ones(ang.shape[:-1] + (rest,), F32)
    zeros = jnp.zeros(ang.shape[:-1] + (rest,), F32)
    reps = LANES // ATTN_HEAD_DIM
    tile = lambda parts: jnp.tile(jnp.concatenate(parts, axis=-1), (1, 1, reps))
    return tile([cos, cos, ones]), tile([-sin, sin, zeros])


def _split_w_in(w):
    ab0 = IN_MAIN - GDN_WIDTH
    ab1 = ab0 + 2 * GDN_HEADS
    w = w.astype(BF16)
    return w[..., :ab0], w[..., ab1:], jnp.pad(w[..., ab0:ab1], ((0, 0), (0, 0), (0, LANES - 2 * GDN_HEADS)))


def _lane_padded(v):
    return jnp.pad(v.astype(F32), ((0, 0), (0, LANES - v.shape[1])))[:, None, :]


def kernel(x, mem, positions, norm_mix_pre, norm_mix_post, w_in, conv_short, conv_gdn, gdn_a_log, gdn_dt_bias, gdn_norm, w_out, norm_mem, norm_xattn_pre, norm_xattn_post, w_xq, w_xkv, w_xo, norm_ffn_pre, norm_ffn_post, w_gate_up, w_down):
    b, s, d = x.shape
    m = mem.shape[1]
    depth = w_in.shape[0]
    cos_t, sin_t = (t.reshape(b * s, LANES) for t in _rotary_tables(positions))
    row = lambda g: g[:, None, :]
    (w_in16, w_gate16, w_ab16), w_xkv16 = _split_w_in(w_in), w_xkv.astype(BF16)
    alog_v, dtb_v = _lane_padded(gdn_a_log), _lane_padded(gdn_dt_bias)
    mem2 = mem.reshape(b * m, d)
    weights = (w_out.astype(BF16), w_xq.astype(BF16), w_xo.astype(BF16), w_gate_up.astype(BF16), w_down.astype(BF16))
    gains = tuple(row(g) for g in (norm_mix_post, norm_xattn_pre, norm_xattn_post, norm_ffn_pre, norm_ffn_post))
    h = x
    for l in range(depth):
        planes, yc, qkv_g, gate, ab = in_proj(h.reshape(b * s, d), row(norm_mix_pre), w_in16, w_gate16, w_ab16,
                                              conv_short, conv_gdn, cos_t, sin_t, l, s)
        ya = dilated_attention(planes, b)
        yg = gated_deltanet(qkv_g.reshape(b, s, -1), gate.reshape(b, s, -1), ab.reshape(b, s, -1),
                            alog_v, dtb_v, row(gdn_norm), l)
        kv = norm_matmul(mem2, row(norm_mem), w_xkv16, l, out_dtype=BF16).reshape(b, m, 2 * d)
        h = post_mixer(ya, yc.reshape(b, s, -1), yg, h, kv, weights, gains, l)
    return h
```
